```python
import math, functools
import jax, jax.numpy as jnp
from jax import lax
import numpy as np


D_MODEL = 2048
BATCH = 4
SEQ = 2048
DEPTH = 1
DEC_BATCH = 32
DEC_SEQ = 8
PAST_LEN = 16384
PAGE_SIZE = 128

DA_WIDTH = D_MODEL // 2
DA_HEAD_DIM = 64
DA_VDIM = 2 * DA_HEAD_DIM
DA_HEADS = DA_WIDTH // DA_VDIM
HG_WIDTH = D_MODEL // 2
HG_EXPAND = 128
HG_HEADS = HG_WIDTH // HG_EXPAND
HG_DK = HG_EXPAND
HG_DV = HG_WIDTH // HG_HEADS
HG_CHUNK = 64
D_FF = 4 * D_MODEL
Q_BLOCK = 128
LN_EPS = 1e-5
RMS_EPS = 1e-6
ALPHA = (2 * DEPTH) ** 0.25
BETA = (8 * DEPTH) ** -0.25
IN_SPLITS = (DA_HEADS * 2 * DA_HEAD_DIM, DA_HEADS * 2 * DA_HEAD_DIM, DA_WIDTH,
             HG_HEADS * HG_DK, HG_HEADS * HG_DK, HG_HEADS * HG_DV, HG_WIDTH,
             D_MODEL, D_MODEL)
IN_WIDTH = sum(IN_SPLITS)

kernel_name = 'hgrn2_diffattn_gated_hybrid_step'


def _layernorm(x, w, b):
    xf = x.astype(jnp.float32)
    mu = jnp.mean(xf, axis=-1, keepdims=True)
    var = jnp.mean(jnp.square(xf - mu), axis=-1, keepdims=True)
    return ((xf - mu) * lax.rsqrt(var + LN_EPS) * w + b).astype(x.dtype)


def _rmsnorm(x, w):
    xf = x.astype(jnp.float32)
    return xf * lax.rsqrt(jnp.mean(jnp.square(xf), axis=-1, keepdims=True) + RMS_EPS) * w


def _hgrn2_chunked(q, k, v, logf, s0):
    B, L = q.shape[0], q.shape[1]
    C = L if L <= HG_CHUNK else HG_CHUNK
    n = L // C

    def to_chunks(a):
        return a.reshape(B, n, C, *a.shape[2:]).swapaxes(0, 1)

    causal = jnp.tril(jnp.ones((C, C), dtype=bool))[None, :, :, None, None]

    def step(S, inp):
        qc, kc, vc, gc = inp
        b = jnp.cumsum(gc, axis=1)
        diff = b[:, :, None] - b[:, None, :]
        decay = jnp.where(causal, jnp.exp(jnp.minimum(diff, 0.0)), 0.0)
        scores = jnp.einsum('bthk,bshk,btshk->bhts', qc, kc, decay)
        o = jnp.einsum('bhts,bshv->bthv', scores, vc) + jnp.einsum('bthk,bhkv->bthv', qc * jnp.exp(b), S)
        b_last = b[:, -1]
        kd = kc * jnp.exp(b_last[:, None] - b)
        S_new = jnp.exp(b_last)[..., None] * S + jnp.einsum('bshk,bshv->bhkv', kd, vc)
        return S_new, o

    S, o = lax.scan(step, s0, (to_chunks(q), to_chunks(k), to_chunks(v), to_chunks(logf)))
    o = o.swapaxes(0, 1).reshape(B, L, q.shape[2], v.shape[-1])
    return o, S


def _diff_attn_prompt(q, k, v, lam):
    B, S = q.shape[0], q.shape[1]
    nb = S // Q_BLOCK
    qf = q.astype(jnp.float32) * (DA_HEAD_DIM ** -0.5)
    kf = k.astype(jnp.float32)
    vf = v.astype(jnp.float32)
    qb = qf.reshape(B, nb, Q_BLOCK, *q.shape[2:]).swapaxes(0, 1)
    pos_k = jnp.arange(S)

    def block(args):
        qi, i = args
        s = jnp.einsum('bqhmd,bkhmd->bhmqk', qi, kf)
        pos_q = i * Q_BLOCK + jnp.arange(Q_BLOCK)
        s = jnp.where(pos_k[None, :] <= pos_q[:, None], s, -jnp.inf)
        p = jax.nn.softmax(s, axis=-1)
        return jnp.einsum('bhmqk,bkhe->bqhme', p, vf)

    o = lax.map(block, (qb, jnp.arange(nb)))
    o = o.swapaxes(0, 1).reshape(B, S, q.shape[2], 2, v.shape[-1])
    return o[..., 0, :] - lam * o[..., 1, :]


def _online_update(carry, q, k, v, mask):
    m, l, acc = carry
    s = jnp.einsum('bthmd,bshmd->bhmts', q, k)
    if mask is not None:
        s = jnp.where(mask, s, -jnp.inf)
    m_new = jnp.maximum(m, jnp.max(s, axis=-1))
    corr = jnp.exp(m - m_new)
    p = jnp.exp(s - m_new[..., None])
    l_new = l * corr + jnp.sum(p, axis=-1)
    acc_new = acc * corr[..., None] + jnp.einsum('bhmts,bshe->bhmte', p, v)
    return (m_new, l_new, acc_new)


def _diff_attn_sample(q, k, v, lam, cache_k, cache_v, page_table, layer):
    DB, T = q.shape[0], q.shape[1]
    qf = q.astype(jnp.float32) * (DA_HEAD_DIM ** -0.5)
    init = (jnp.full((DB, DA_HEADS, 2, T), -jnp.inf, jnp.float32),
            jnp.zeros((DB, DA_HEADS, 2, T), jnp.float32),
            jnp.zeros((DB, DA_HEADS, 2, T, DA_VDIM), jnp.float32))

    def page_step(carry, phys):
        kp = cache_k[layer, phys].astype(jnp.float32)
        vp = cache_v[layer, phys].astype(jnp.float32)
        return _online_update(carry, qf, kp, vp, None), None

    carry, _ = lax.scan(page_step, init, page_table.T)
    causal = jnp.tril(jnp.ones((T, T), dtype=bool))
    m, l, acc = _online_update(carry, qf, k.astype(jnp.float32), v.astype(jnp.float32), causal)
    o = acc / l[..., None]
    o = o[:, :, 0] - lam * o[:, :, 1]
    return o.transpose(0, 2, 1, 3)


def _block(x, s0, attend, layer, w_in, hg_lb, hg_norm_w, lam_q1, lam_k1, lam_q2, lam_k2, da_norm_w,
           w_pa, w_pb, w_out, ln1_w, ln1_b, w_up, w_down, ln2_w, ln2_b):
    B, L, _ = x.shape
    idx = np.cumsum(IN_SPLITS)[:-1].tolist()
    dq, dk, dv, hq, hf, hi, hgate, ga, gb = jnp.split(x @ w_in, idx, axis=-1)

    lb = jnp.cumsum(jax.nn.softmax(hg_lb.astype(jnp.float32), axis=0), axis=0)[layer]
    z = hf.astype(jnp.float32)
    logf = jnp.log(lb + (1.0 - lb) * jax.nn.sigmoid(z))
    kk = (1.0 - lb) * jax.nn.sigmoid(-z)
    qq = jax.nn.silu(hq.astype(jnp.float32))
    shp_k = (B, L, HG_HEADS, HG_DK)
    shp_v = (B, L, HG_HEADS, HG_DV)
    o_hg, s_new = _hgrn2_chunked(qq.reshape(shp_k), kk.reshape(shp_k), hi.astype(jnp.float32).reshape(shp_v),
                                 logf.reshape(shp_k), s0.astype(jnp.float32))
    o_hg = _rmsnorm(o_hg, hg_norm_w) * jax.nn.sigmoid(hgate.astype(jnp.float32).reshape(shp_v))
    o_hg = o_hg.reshape(B, L, HG_WIDTH).astype(x.dtype)

    q = dq.reshape(B, L, DA_HEADS, 2, DA_HEAD_DIM)
    k = dk.reshape(B, L, DA_HEADS, 2, DA_HEAD_DIM)
    v = dv.reshape(B, L, DA_HEADS, DA_VDIM)
    lam_init = 0.8 - 0.6 * math.exp(-0.3 * layer)
    lam = (jnp.exp(jnp.sum(lam_q1.astype(jnp.float32) * lam_k1.astype(jnp.float32)))
           - jnp.exp(jnp.sum(lam_q2.astype(jnp.float32) * lam_k2.astype(jnp.float32))) + lam_init)
    o_da = attend(q, k, v, lam)
    o_da = (_rmsnorm(o_da, da_norm_w) * (1.0 - lam_init)).reshape(B, L, DA_WIDTH).astype(x.dtype)

    merged = jax.nn.sigmoid(ga) * (o_hg @ w_pa) + jax.nn.sigmoid(gb) * (o_da @ w_pb)
    h = _layernorm(ALPHA * x + merged @ w_out, ln1_w, ln1_b)

    f = jnp.square(jax.nn.relu(h @ w_up)) @ w_down
    y = _layernorm(ALPHA * h + f, ln2_w, ln2_b)
    return y, k, v, s_new


def setup_inputs(seed: int = 0) -> dict:
    key = jax.random.key(seed)
    ks = jax.random.split(key, 24)
    f32 = jnp.float32
    n_pages = PAST_LEN // PAGE_SIZE
    n_phys = (DEC_BATCH * n_pages * 5) // 4

    def nrm(k, shape, scale):
        return jax.random.normal(k, shape, f32) * scale

    col_scales = (1.0, 1.0, BETA, 1.0, 1.0, BETA, 1.0, 1.0, 1.0)
    col_scale = jnp.concatenate([jnp.full((n,), s, f32) for n, s in zip(IN_SPLITS, col_scales)])
    page_table = jax.random.permutation(ks[5], n_phys)[:DEC_BATCH * n_pages].reshape(DEC_BATCH, n_pages).astype(jnp.int32)
    return {
        'x_prompt': nrm(ks[0], (BATCH, SEQ, D_MODEL), 1.0),
        'x_sample': nrm(ks[1], (DEC_BATCH, DEC_SEQ, D_MODEL), 1.0),
        'cache_k': nrm(ks[2], (DEPTH, n_phys, PAGE_SIZE, DA_HEADS, 2, DA_HEAD_DIM), 1.0),
        'cache_v': nrm(ks[3], (DEPTH, n_phys, PAGE_SIZE, DA_HEADS, DA_VDIM), 0.5),
        'state_hgrn': nrm(ks[4], (DEPTH, DEC_BATCH, HG_HEADS, HG_DK, HG_DV), 0.5),
        'page_table': page_table,
        'w_in': nrm(ks[6], (DEPTH, D_MODEL, IN_WIDTH), D_MODEL ** -0.5) * col_scale,
        'hg_lb': nrm(ks[7], (DEPTH + 1, HG_HEADS * HG_DK), 0.1),
        'hg_norm_w': 1.0 + nrm(ks[8], (DEPTH, HG_DV), 0.02),
        'lam_q1': nrm(ks[9], (DEPTH, DA_HEAD_DIM), 0.1),
        'lam_k1': nrm(ks[10], (DEPTH, DA_HEAD_DIM), 0.1),
        'lam_q2': nrm(ks[11], (DEPTH, DA_HEAD_DIM), 0.1),
        'lam_k2': nrm(ks[12], (DEPTH, DA_HEAD_DIM), 0.1),
        'da_norm_w': 1.0 + nrm(ks[13], (DEPTH, DA_VDIM), 0.02),
        'w_pa': nrm(ks[14], (DEPTH, HG_WIDTH, D_MODEL), HG_WIDTH ** -0.5),
        'w_pb': nrm(ks[15], (DEPTH, DA_WIDTH, D_MODEL), DA_WIDTH ** -0.5),
        'w_out': nrm(ks[16], (DEPTH, D_MODEL, D_MODEL), D_MODEL ** -0.5 * BETA),
        'ln1_w': 1.0 + nrm(ks[17], (DEPTH, D_MODEL), 0.02),
        'ln1_b': nrm(ks[18], (DEPTH, D_MODEL), 0.02),
        'w_up': nrm(ks[19], (DEPTH, D_MODEL, D_FF), D_MODEL ** -0.5 * BETA),
        'w_down': nrm(ks[20], (DEPTH, D_FF, D_MODEL), D_FF ** -0.5 * BETA),
        'ln2_w': 1.0 + nrm(ks[21], (DEPTH, D_MODEL), 0.02),
        'ln2_b': nrm(ks[22], (DEPTH, D_MODEL), 0.02),
    }


def reference(x_prompt, x_sample, cache_k, cache_v, state_hgrn, page_table, w_in, hg_lb, hg_norm_w,
              lam_q1, lam_k1, lam_q2, lam_k2, da_norm_w, w_pa, w_pb, w_out, ln1_w, ln1_b,
              w_up, w_down, ln2_w, ln2_b):
    y_p = x_prompt
    y_s = x_sample
    kp_list, vp_list, sp_list, ks_list, vs_list, ss_list = [], [], [], [], [], []
    for layer in range(DEPTH):
        ws = (w_in[layer], hg_lb, hg_norm_w[layer], lam_q1[layer], lam_k1[layer], lam_q2[layer], lam_k2[layer],
              da_norm_w[layer], w_pa[layer], w_pb[layer], w_out[layer], ln1_w[layer], ln1_b[layer],
              w_up[layer], w_down[layer], ln2_w[layer], ln2_b[layer])
        s0_p = jnp.zeros((BATCH, HG_HEADS, HG_DK, HG_DV), jnp.float32)
        y_p, k_p, v_p, s_p = _block(y_p, s0_p, _diff_attn_prompt, layer, *ws)
        attend_s = functools.partial(_diff_attn_sample, cache_k=cache_k, cache_v=cache_v,
                                     page_table=page_table, layer=layer)
        y_s, k_s, v_s, s_s = _block(y_s, state_hgrn[layer], attend_s, layer, *ws)
        kp_list.append(k_p); vp_list.append(v_p); sp_list.append(s_p.astype(x_prompt.dtype))
        ks_list.append(k_s); vs_list.append(v_s); ss_list.append(s_s.astype(x_sample.dtype))
    k_prompt = jnp.stack(kp_list)
    v_prompt = jnp.stack(vp_list)
    hgrn_prompt = jnp.stack(sp_list)
    k_sample = jnp.stack(ks_list)
    v_sample = jnp.stack(vs_list)
    hgrn_sample = jnp.stack(ss_list)
    return (y_p, y_s, k_prompt, v_prompt, hgrn_prompt, k_sample, v_sample, hgrn_sample)
```

```python
import functools
import math

import jax
import jax.numpy as jnp
from jax import lax
from jax.experimental import pallas as pl
from jax.experimental.pallas import tpu as pltpu

D_MODEL = 2048
HEADS = 8
HEAD_W = 128
DA_HEAD_DIM = 64
BRANCH_W = HEADS * HEAD_W
D_FF = 4 * D_MODEL
PAGE = 128
LN_EPS = 1e-5
RMS_EPS = 1e-6
DEPTH = 1
ALPHA = (2 * DEPTH) ** 0.25
LAM_INIT = 0.8 - 0.6 * math.exp(-0.3 * 0)
IN_WIDTH = 7 * BRANCH_W + 2 * D_MODEL
COL_Q, COL_K, COL_V, COL_HQ, COL_HF, COL_HI, COL_HG = (i * HEADS for i in range(7))
N_MAIN = 7 * BRANCH_W
NEG = -1e30

F32 = jnp.float32
BF16 = jnp.bfloat16
VMEM_LIMIT = 52 * 1024 * 1024


def _cparams(sem):
    return pltpu.CompilerParams(dimension_semantics=sem, vmem_limit_bytes=VMEM_LIMIT)


def _sigmoid(x):
    return 1.0 / (1.0 + jnp.exp(-x))


def _matmul_kernel(x_ref, w_ref, o_ref):
    o_ref[...] = jnp.dot(x_ref[...], w_ref[...], preferred_element_type=F32)


def _in_proj(x_bf, w_bf, tm, tn):
    t, k = x_bf.shape
    n = w_bf.shape[1]
    return pl.pallas_call(
        _matmul_kernel,
        grid=(n // tn, t // tm),
        in_specs=[pl.BlockSpec((tm, k), lambda j, i: (i, 0)),
                  pl.BlockSpec((k, tn), lambda j, i: (0, j))],
        out_specs=pl.BlockSpec((tm, tn), lambda j, i: (i, j)),
        out_shape=jax.ShapeDtypeStruct((t, n), F32),
        compiler_params=_cparams(("parallel", "arbitrary")),
        name="in_proj",
    )(x_bf, w_bf)


def _hgrn_kernel(*refs, tb, c, has_s0):
    if has_s0:
        hq_ref, hf_ref, hi_ref, hg_ref, lb_ref, nw_ref, s0_ref, o_ref, s_ref, st_scr = refs
    else:
        hq_ref, hf_ref, hi_ref, hg_ref, lb_ref, nw_ref, o_ref, s_ref, st_scr = refs
        s0_ref = None
    t = pl.program_id(2)

    @pl.when(t == 0)
    def _():
        if has_s0:
            st_scr[...] = s0_ref[...].T
        else:
            st_scr[...] = jnp.zeros_like(st_scr)

    lbx = lb_ref[...]
    lbe = jnp.exp(lbx - jnp.max(lbx, axis=0, keepdims=True))
    lb = lbe[0:1, :] / jnp.sum(lbe, axis=0, keepdims=True)
    one_m_lb = 1.0 - lb
    nw = nw_ref[...]

    row = lax.broadcasted_iota(jnp.int32, (c, c), 0)
    col = lax.broadcasted_iota(jnp.int32, (c, c), 1)
    tri = jnp.where(row >= col, 1.0, 0.0).astype(BF16)
    causal = row >= col

    def chunk(ci, st):
        r = pl.ds(pl.multiple_of(ci * c, c), c)
        z = hf_ref[r, :]
        logf = jnp.log(lb + one_m_lb * _sigmoid(z))
        kk = one_m_lb * _sigmoid(-z)
        hq = hq_ref[r, :]
        qq = hq * _sigmoid(hq)
        v = hi_ref[r, :]
        p0 = logf.astype(BF16)
        r1 = logf - p0.astype(F32)
        p1 = r1.astype(BF16)
        p2 = (r1 - p1.astype(F32)).astype(BF16)
        b = (jnp.dot(tri, p0, preferred_element_type=F32)
             + jnp.dot(tri, p1, preferred_element_type=F32)
             + jnp.dot(tri, p2, preferred_element_type=F32))
        b_last = b[c - 1:c, :]
        qe = qq * jnp.exp(b)
        kd = kk * jnp.exp(b_last - b)
        a = jnp.zeros((c, c), F32)
        for s in range(c):
            lo = (s // 8) * 8
            d = jnp.minimum(b[lo:, :] - b[s:s + 1, :], 0.0)
            term = qq[lo:, :] * kk[s:s + 1, :] * jnp.exp(d)
            colsum = jnp.sum(term, axis=-1, keepdims=True)
            if lo:
                colsum = jnp.concatenate([jnp.zeros((lo, 1), F32), colsum], axis=0)
            a = jnp.where(col == s, colsum, a)
        a = jnp.where(causal, a, 0.0)
        v_bf = v.astype(BF16)
        o = (jnp.dot(a.astype(BF16), v_bf, preferred_element_type=F32)
             + lax.dot_general(qe.astype(BF16), st.astype(BF16), (((1,), (1,)), ((), ())),
                               preferred_element_type=F32))
        u_t = lax.dot_general(v_bf, kd.astype(BF16), (((0,), (0,)), ((), ())),
                              preferred_element_type=F32)
        st_new = st * jnp.exp(b_last) + u_t
        rms = lax.rsqrt(jnp.mean(o * o, axis=-1, keepdims=True) + RMS_EPS)
        o_ref[r, :] = o * rms * nw * _sigmoid(hg_ref[r, :])
        return st_new

    st = lax.fori_loop(0, tb // c, chunk, st_scr[...])
    st_scr[...] = st

    @pl.when(t == pl.num_programs(2) - 1)
    def _():
        s_ref[...] = st.T


def _hgrn(proj, hg_lb, hg_norm_w, s0, nb, seq, tb, c):
    nt = seq // tb
    has_s0 = s0 is not None

    def tok(colbase):
        return pl.BlockSpec((tb, HEAD_W), lambda b, h, t: (b * nt + t, colbase + h))

    in_specs = [tok(COL_HQ), tok(COL_HF), tok(COL_HI), tok(COL_HG),
                pl.BlockSpec((hg_lb.shape[0], HEAD_W), lambda b, h, t: (0, h)),
                pl.BlockSpec((1, HEAD_W), lambda b, h, t: (0, 0))]
    args = [proj, proj, proj, proj, hg_lb, hg_norm_w]
    if has_s0:
        in_specs.append(pl.BlockSpec((None, None, HEAD_W, HEAD_W), lambda b, h, t: (b, h, 0, 0)))
        args.append(s0)
    return pl.pallas_call(
        functools.partial(_hgrn_kernel, tb=tb, c=c, has_s0=has_s0),
        grid=(nb, HEADS, nt),
        in_specs=in_specs,
        out_specs=[pl.BlockSpec((tb, HEAD_W), lambda b, h, t: (b * nt + t, h)),
                   pl.BlockSpec((None, None, HEAD_W, HEAD_W), lambda b, h, t: (b, h, 0, 0))],
        out_shape=[jax.ShapeDtypeStruct((nb * seq, BRANCH_W), F32),
                   jax.ShapeDtypeStruct((nb, HEADS, HEAD_W, HEAD_W), F32)],
        scratch_shapes=[pltpu.VMEM((HEAD_W, HEAD_W), F32)],
        compiler_params=_cparams(("parallel", "parallel", "arbitrary")),
        name="hgrn_s0" if has_s0 else "hgrn",
    )(*args)


def _lam(q1_ref, k1_ref, q2_ref, k2_ref):
    s1 = jnp.sum(q1_ref[...] * k1_ref[...], axis=-1, keepdims=True)
    s2 = jnp.sum(q2_ref[...] * k2_ref[...], axis=-1, keepdims=True)
    return jnp.exp(s1) - jnp.exp(s2) + LAM_INIT


def _da_finish(o1, o2, lam, nw):
    o = o1 - lam * o2
    rms = lax.rsqrt(jnp.mean(o * o, axis=-1, keepdims=True) + RMS_EPS)
    return o * rms * nw * (1.0 - LAM_INIT)


def _attn_prompt_kernel(q_ref, k_ref, v_ref, lq1, lk1, lq2, lk2, nw_ref, o_ref, kb, vb, *, tq, tk):
    qi = pl.program_id(2)

    @pl.when(qi == 0)
    def _():
        kb[...] = k_ref[...].astype(BF16)
        vb[...] = v_ref[...].astype(BF16)

    q = q_ref[...] * (DA_HEAD_DIM ** -0.5)
    lane = lax.broadcasted_iota(jnp.int32, q.shape, 1)
    qa = jnp.where(lane < DA_HEAD_DIM, q, 0.0).astype(BF16)
    qb = jnp.where(lane >= DA_HEAD_DIM, q, 0.0).astype(BF16)
    row = lax.broadcasted_iota(jnp.int32, (tq, tk), 0) + qi * tq
    col0 = lax.broadcasted_iota(jnp.int32, (tq, tk), 1)

    def body(j, carry):
        m1, l1, a1, m2, l2, a2 = carry
        ks = pl.ds(pl.multiple_of(j * tk, tk), tk)
        kj = kb[ks, :]
        vj = vb[ks, :]
        ok = (col0 + j * tk) <= row
        outs = []
        for qm, m, l, a in ((qa, m1, l1, a1), (qb, m2, l2, a2)):
            s = lax.dot_general(qm, kj, (((1,), (1,)), ((), ())), preferred_element_type=F32)
            s = jnp.where(ok, s, NEG)
            m_new = jnp.maximum(m, jnp.max(s, axis=-1, keepdims=True))
            corr = jnp.exp(m - m_new)
            p = jnp.exp(s - m_new)
            l_new = l * corr + jnp.sum(p, axis=-1, keepdims=True)
            a_new = a * corr + jnp.dot(p.astype(BF16), vj, preferred_element_type=F32)
            outs += [m_new, l_new, a_new]
        return tuple(outs)

    init = (jnp.full((tq, 1), NEG, F32), jnp.zeros((tq, 1), F32), jnp.zeros((tq, HEAD_W), F32)) * 2
    nkv = (qi * tq) // tk + tq // tk
    m1, l1, a1, m2, l2, a2 = lax.fori_loop(0, nkv, body, init)
    lam = _lam(lq1, lk1, lq2, lk2)
    o_ref[...] = _da_finish(a1 / l1, a2 / l2, lam, nw_ref[...])


def _attn_prompt(proj, lam_q1, lam_k1, lam_q2, lam_k2, da_norm_w, nb, seq, tq, tk):
    nq = seq // tq
    small = lambda shp: pl.BlockSpec(shp, lambda b, h, i: (0, 0))
    return pl.pallas_call(
        functools.partial(_attn_prompt_kernel, tq=tq, tk=tk),
        grid=(nb, HEADS, nq),
        in_specs=[pl.BlockSpec((tq, HEAD_W), lambda b, h, i: (b * nq + i, COL_Q + h)),
                  pl.BlockSpec((seq, HEAD_W), lambda b, h, i: (b, COL_K + h)),
                  pl.BlockSpec((seq, HEAD_W), lambda b, h, i: (b, COL_V + h)),
                  small((1, DA_HEAD_DIM)), small((1, DA_HEAD_DIM)),
                  small((1, DA_HEAD_DIM)), small((1, DA_HEAD_DIM)),
                  small((1, HEAD_W))],
        out_specs=pl.BlockSpec((tq, HEAD_W), lambda b, h, i: (b * nq + i, h)),
        out_shape=jax.ShapeDtypeStruct((nb * seq, BRANCH_W), F32),
        scratch_shapes=[pltpu.VMEM((seq, HEAD_W), BF16), pltpu.VMEM((seq, HEAD_W), BF16)],
        compiler_params=_cparams(("parallel", "parallel", "arbitrary")),
        name="attn_prompt",
    )(proj, proj, proj, lam_q1, lam_k1, lam_q2, lam_k2, da_norm_w)


def _attn_sample_kernel(pt_ref, q_ref, kn_ref, vn_ref, *rest, pp, tnew):
    k_refs = rest[:pp]
    v_refs = rest[pp:2 * pp]
    lq1, lk1, lq2, lk2, nw_ref, o_ref, qblk, kcat, vcat, m_scr, l_scr, acc = rest[2 * pp:]
    g = pl.program_id(1)
    rows = HEADS * 2 * tnew

    @pl.when(g == 0)
    def _():
        q = q_ref[...] * (DA_HEAD_DIM ** -0.5)
        tiled = jnp.concatenate([q] * (HEADS * 2), axis=0)
        r = lax.broadcasted_iota(jnp.int32, tiled.shape, 0)
        cidx = lax.broadcasted_iota(jnp.int32, tiled.shape, 1)
        qblk[...] = jnp.where((r // tnew) == (cidx // DA_HEAD_DIM), tiled, 0.0).astype(BF16)
        m_scr[...] = jnp.full_like(m_scr, NEG)
        l_scr[...] = jnp.zeros_like(l_scr)
        acc[...] = jnp.zeros_like(acc)

    def update(s, kv_len):
        m_old = m_scr[...]
        m_new = jnp.maximum(m_old, jnp.max(s, axis=-1, keepdims=True))
        corr = jnp.exp(m_old - m_new)
        p = jnp.exp(s - m_new)
        l_scr[...] = l_scr[...] * corr + jnp.sum(p, axis=-1, keepdims=True)
        m_scr[...] = m_new
        p_bf = p.astype(BF16)
        for h in range(HEADS):
            rs = slice(h * 2 * tnew, (h + 1) * 2 * tnew)
            pv = jnp.dot(p_bf[rs, :], vcat[0:kv_len, h * HEAD_W:(h + 1) * HEAD_W],
                         preferred_element_type=F32)
            acc[rs, :] = acc[rs, :] * corr[rs, :] + pv

    for j in range(pp):
        kcat[j * PAGE:(j + 1) * PAGE, :] = k_refs[j][...].astype(BF16)
        vcat[j * PAGE:(j + 1) * PAGE, :] = v_refs[j][...].astype(BF16)
    s = lax.dot_general(qblk[...], kcat[...], (((1,), (1,)), ((), ())), preferred_element_type=F32)
    update(s, pp * PAGE)

    @pl.when(g == pl.num_programs(1) - 1)
    def _():
        pad = jnp.zeros((PAGE - tnew, BRANCH_W), F32)
        kcat[0:PAGE, :] = jnp.concatenate([kn_ref[...], pad], axis=0).astype(BF16)
        vcat[0:PAGE, :] = jnp.concatenate([vn_ref[...], pad], axis=0).astype(BF16)
        sn = lax.dot_general(qblk[...], kcat[0:PAGE, :], (((1,), (1,)), ((), ())),
                             preferred_element_type=F32)
        r = lax.broadcasted_iota(jnp.int32, sn.shape, 0)
        cidx = lax.broadcasted_iota(jnp.int32, sn.shape, 1)
        sn = jnp.where(cidx <= (r % tnew), sn, NEG)
        update(sn, PAGE)
        lam = _lam(lq1, lk1, lq2, lk2)
        o = acc[...] / l_scr[...]
        nw = nw_ref[...]
        for h in range(HEADS):
            o1 = o[h * 2 * tnew:h * 2 * tnew + tnew, :]
            o2 = o[h * 2 * tnew + tnew:(h + 1) * 2 * tnew, :]
            o_ref[:, h * HEAD_W:(h + 1) * HEAD_W] = _da_finish(o1, o2, lam, nw)


def _attn_sample(proj, cache_k, cache_v, page_table, lam_q1, lam_k1, lam_q2, lam_k2, da_norm_w,
                 nb, tnew, pp):
    n_pages = page_table.shape[1]
    ck = cache_k.reshape(-1, PAGE, BRANCH_W)
    cv = cache_v.reshape(-1, PAGE, BRANCH_W)
    rows = HEADS * 2 * tnew

    def page_spec(j):
        return pl.BlockSpec((None, PAGE, BRANCH_W), lambda b, g, pt: (pt[b, g * pp + j], 0, 0))

    small = lambda shp: pl.BlockSpec(shp, lambda b, g, pt: (0, 0))
    grid_spec = pltpu.PrefetchScalarGridSpec(
        num_scalar_prefetch=1,
        grid=(nb, n_pages // pp),
        in_specs=[pl.BlockSpec((tnew, BRANCH_W), lambda b, g, pt: (b, 0)),
                  pl.BlockSpec((tnew, BRANCH_W), lambda b, g, pt: (b, 1)),
                  pl.BlockSpec((tnew, BRANCH_W), lambda b, g, pt: (b, 2))]
                 + [page_spec(j) for j in range(pp)] + [page_spec(j) for j in range(pp)]
                 + [small((1, DA_HEAD_DIM))] * 4 + [small((1, HEAD_W))],
        out_specs=pl.BlockSpec((tnew, BRANCH_W), lambda b, g, pt: (b, 0)),
        scratch_shapes=[pltpu.VMEM((rows, BRANCH_W), BF16),
                        pltpu.VMEM((pp * PAGE, BRANCH_W), BF16),
                        pltpu.VMEM((pp * PAGE, BRANCH_W), BF16),
                        pltpu.VMEM((rows, 1), F32),
                        pltpu.VMEM((rows, 1), F32),
                        pltpu.VMEM((rows, HEAD_W), F32)],
    )
    return pl.pallas_call(
        functools.partial(_attn_sample_kernel, pp=pp, tnew=tnew),
        grid_spec=grid_spec,
        out_shape=jax.ShapeDtypeStruct((nb * tnew, BRANCH_W), F32),
        compiler_params=_cparams(("parallel", "arbitrary")),
        name="attn_sample",
    )(page_table, proj, proj, proj, *([ck] * pp), *([cv] * pp),
      lam_q1, lam_k1, lam_q2, lam_k2, da_norm_w)


def _layernorm(x, w, b):
    mu = jnp.mean(x, axis=-1, keepdims=True)
    xc = x - mu
    var = jnp.mean(xc * xc, axis=-1, keepdims=True)
    return xc * lax.rsqrt(var + LN_EPS) * w + b


def _merge_kernel(x_ref, ohg_ref, oda_ref, ga_ref, gb_ref, wpa_ref, wpb_ref, wout_ref,
                  lw_ref, lb_ref, h_ref):
    pa = jnp.dot(ohg_ref[...].astype(BF16), wpa_ref[...], preferred_element_type=F32)
    pb = jnp.dot(oda_ref[...].astype(BF16), wpb_ref[...], preferred_element_type=F32)
    merged = _sigmoid(ga_ref[...]) * pa + _sigmoid(gb_ref[...]) * pb
    y = ALPHA * x_ref[...] + jnp.dot(merged.astype(BF16), wout_ref[...], preferred_element_type=F32)
    h_ref[...] = _layernorm(y, lw_ref[...], lb_ref[...])


def _merge(x, o_hg, o_da, gates, wpa, wpb, wout, ln_w, ln_b, tm):
    t = x.shape[0]
    rowblk = lambda w: pl.BlockSpec((tm, w), lambda i: (i, 0))
    const = lambda shp: pl.BlockSpec(shp, lambda i: (0, 0), pipeline_mode=pl.Buffered(1))
    return pl.pallas_call(
        _merge_kernel,
        grid=(t // tm,),
        in_specs=[rowblk(D_MODEL), rowblk(BRANCH_W), rowblk(BRANCH_W),
                  pl.BlockSpec((tm, D_MODEL), lambda i: (i, 0)),
                  pl.BlockSpec((tm, D_MODEL), lambda i: (i, 1)),
                  const((BRANCH_W, D_MODEL)), const((BRANCH_W, D_MODEL)), const((D_MODEL, D_MODEL)),
                  const((1, D_MODEL)), const((1, D_MODEL))],
        out_specs=rowblk(D_MODEL),
        out_shape=jax.ShapeDtypeStruct((t, D_MODEL), F32),
        compiler_params=_cparams(("parallel",)),
        name="merge_out_ln",
    )(x, o_hg, o_da, gates, gates, wpa, wpb, wout, ln_w, ln_b)


def _mlp_kernel(h_ref, wup_ref, wdn_ref, lw_ref, lb_ref, y_ref, hb):
    j = pl.program_id(1)

    @pl.when(j == 0)
    def _():
        hb[...] = h_ref[...].astype(BF16)
        y_ref[...] = jnp.zeros_like(y_ref)

    u = jnp.dot(hb[...], wup_ref[...], preferred_element_type=F32)
    u = jnp.maximum(u, 0.0)
    y_ref[...] += jnp.dot((u * u).astype(BF16), wdn_ref[...], preferred_element_type=F32)

    @pl.when(j == pl.num_programs(1) - 1)
    def _():
        y_ref[...] = _layernorm(ALPHA * h_ref[...] + y_ref[...], lw_ref[...], lb_ref[...])


def _mlp(h, wup, wdn, ln_w, ln_b, tm, tf):
    t = h.shape[0]
    return pl.pallas_call(
        _mlp_kernel,
        grid=(t // tm, D_FF // tf),
        in_specs=[pl.BlockSpec((tm, D_MODEL), lambda i, j: (i, 0), pipeline_mode=pl.Buffered(1)),
                  pl.BlockSpec((D_MODEL, tf), lambda i, j: (0, j)),
                  pl.BlockSpec((tf, D_MODEL), lambda i, j: (j, 0)),
                  pl.BlockSpec((1, D_MODEL), lambda i, j: (0, 0)),
                  pl.BlockSpec((1, D_MODEL), lambda i, j: (0, 0))],
        out_specs=pl.BlockSpec((tm, D_MODEL), lambda i, j: (i, 0)),
        out_shape=jax.ShapeDtypeStruct((t, D_MODEL), F32),
        scratch_shapes=[pltpu.VMEM((tm, D_MODEL), BF16)],
        compiler_params=_cparams(("parallel", "arbitrary")),
        name="mlp_ln",
    )(h, wup, wdn, ln_w, ln_b)


def kernel(x_prompt, x_sample, cache_k, cache_v, state_hgrn, page_table, w_in, hg_lb, hg_norm_w,
           lam_q1, lam_k1, lam_q2, lam_k2, da_norm_w, w_pa, w_pb, w_out, ln1_w, ln1_b,
           w_up, w_down, ln2_w, ln2_b):
    nb, seq, _ = x_prompt.shape
    db, tnew, _ = x_sample.shape
    w_main = w_in[0, :, :N_MAIN].astype(BF16)
    w_gate = w_in[0, :, N_MAIN:].astype(BF16)
    wpa = w_pa[0].astype(BF16)
    wpb = w_pb[0].astype(BF16)
    wout = w_out[0].astype(BF16)
    wup = w_up[0].astype(BF16)
    wdn = w_down[0].astype(BF16)

    def trunk(x2d, tm_proj, tm_merge, tm_mlp, attend, hgrn):
        x_bf = x2d.astype(BF16)
        proj = _in_proj(x_bf, w_main, tm_proj, 1024)
        gates = _in_proj(x_bf, w_gate, tm_proj, 1024)
        o_hg, s_new = hgrn(proj)
        o_da = attend(proj)
        h = _merge(x2d, o_hg, o_da, gates, wpa, wpb, wout, ln1_w, ln1_b, tm_merge)
        y = _mlp(h, wup, wdn, ln2_w, ln2_b, tm_mlp, 512)
        return y, proj, s_new

    lam_args = (lam_q1, lam_k1, lam_q2, lam_k2, da_norm_w)

    y_p, proj_p, s_p = trunk(
        x_prompt.reshape(nb * seq, D_MODEL), 1024, 256, 1024,
        lambda pr: _attn_prompt(pr, *lam_args, nb, seq, 512, 512),
        lambda pr: _hgrn(pr, hg_lb, hg_norm_w, None, nb, seq, 512, 16))
    y_s, proj_s, s_s = trunk(
        x_sample.reshape(db * tnew, D_MODEL), 256, 256, 256,
        lambda pr: _attn_sample(pr, cache_k, cache_v, page_table, *lam_args, db, tnew, 8),
        lambda pr: _hgrn(pr, hg_lb, hg_norm_w, state_hgrn.reshape(db, HEADS, HEAD_W, HEAD_W),
                         db, tnew, tnew, tnew))

    def kv(proj, b, l):
        k = proj[:, BRANCH_W:2 * BRANCH_W].reshape(1, b, l, HEADS, 2, DA_HEAD_DIM)
        v = proj[:, 2 * BRANCH_W:3 * BRANCH_W].reshape(1, b, l, HEADS, HEAD_W)
        return k, v

    k_p, v_p = kv(proj_p, nb, seq)
    k_s, v_s = kv(proj_s, db, tnew)
    return (y_p.reshape(nb, seq, D_MODEL), y_s.reshape(db, tnew, D_MODEL),
            k_p, v_p, s_p[None], k_s, v_s, s_s[None])
```

```python
import functools
import math

import jax
import jax.numpy as jnp
from jax import lax
from jax.experimental import pallas as pl
from jax.experimental.pallas import tpu as pltpu

D_MODEL = 2048
HEADS = 8
HEAD_W = 128
DA_HEAD_DIM = 64
BRANCH_W = HEADS * HEAD_W
D_FF = 4 * D_MODEL
PAGE = 128
LN_EPS = 1e-5
RMS_EPS = 1e-6
DEPTH = 1
ALPHA = (2 * DEPTH) ** 0.25
LAM_INIT = 0.8 - 0.6 * math.exp(-0.3 * 0)
IN_WIDTH = 7 * BRANCH_W + 2 * D_MODEL
COL_Q, COL_K, COL_V, COL_HQ, COL_HF, COL_HI, COL_HG = (i * HEADS for i in range(7))
N_MAIN = 7 * BRANCH_W
NEG = -1e30

F32 = jnp.float32
BF16 = jnp.bfloat16
VMEM_LIMIT = 52 * 1024 * 1024


def _cparams(sem):
    return pltpu.CompilerParams(dimension_semantics=sem, vmem_limit_bytes=VMEM_LIMIT)


def _sigmoid(x):
    return 1.0 / (1.0 + jnp.exp(-x))


def _matmul_kernel(x_ref, w_ref, o_ref):
    o_ref[...] = jnp.dot(x_ref[...], w_ref[...], preferred_element_type=F32)


def _in_proj(x_bf, w_bf, tm, tn):
    t, k = x_bf.shape
    n = w_bf.shape[1]
    return pl.pallas_call(
        _matmul_kernel,
        grid=(n // tn, t // tm),
        in_specs=[pl.BlockSpec((tm, k), lambda j, i: (i, 0)),
                  pl.BlockSpec((k, tn), lambda j, i: (0, j))],
        out_specs=pl.BlockSpec((tm, tn), lambda j, i: (i, j)),
        out_shape=jax.ShapeDtypeStruct((t, n), F32),
        compiler_params=_cparams(("parallel", "arbitrary")),
        name="in_proj",
    )(x_bf, w_bf)


def _hgrn_kernel(*refs, nbb, tb, c, has_s0):
    if has_s0:
        hq_ref, hf_ref, hi_ref, hg_ref, lb_ref, nw_ref, s0_ref, o_ref, s_ref, st_scr = refs
    else:
        hq_ref, hf_ref, hi_ref, hg_ref, lb_ref, nw_ref, o_ref, s_ref, st_scr = refs
        s0_ref = None
    t = pl.program_id(1)

    @pl.when(t == 0)
    def _():
        for q in range(nbb):
            for h in range(HEADS):
                if has_s0:
                    st_scr[q * HEADS + h] = s0_ref[q, h].T
                else:
                    st_scr[q * HEADS + h] = jnp.zeros((HEAD_W, HEAD_W), F32)

    lbx = lb_ref[...]
    lbe = jnp.exp(lbx - jnp.max(lbx, axis=0, keepdims=True))
    lb_all = lbe[0:1, :] / jnp.sum(lbe, axis=0, keepdims=True)
    nw_t = jnp.concatenate([nw_ref[...]] * HEADS, axis=1)

    row = lax.broadcasted_iota(jnp.int32, (c, c), 0)
    col = lax.broadcasted_iota(jnp.int32, (c, c), 1)
    tri = jnp.where(row >= col, 1.0, 0.0).astype(BF16)
    causal = row >= col

    one_m_lb = 1.0 - lb_all

    def seq_chunk(r, q):
        z = hf_ref[r, :]
        logf = jnp.log(lb_all + one_m_lb * _sigmoid(z))
        kk = one_m_lb * _sigmoid(-z)
        hq = hq_ref[r, :]
        qq = hq * _sigmoid(hq)
        v_bf = hi_ref[r, :].astype(BF16)
        p0 = logf.astype(BF16)
        r1 = logf - p0.astype(F32)
        p1 = r1.astype(BF16)
        p2 = (r1 - p1.astype(F32)).astype(BF16)
        b = (jnp.dot(tri, p0, preferred_element_type=F32)
             + jnp.dot(tri, p1, preferred_element_type=F32)
             + jnp.dot(tri, p2, preferred_element_type=F32))
        b_last = b[c - 1:c, :]
        decay = jnp.exp(b_last)
        qe = (qq * jnp.exp(b)).astype(BF16)
        kd = (kk * jnp.exp(b_last - b)).astype(BF16)
        a = [jnp.zeros((c, c), F32) for _ in range(HEADS)]
        for s in range(c):
            lo = (s // 8) * 8
            d = jnp.minimum(b[lo:, :] - b[s:s + 1, :], 0.0)
            term = qq[lo:, :] * kk[s:s + 1, :] * jnp.exp(d)
            for h in range(HEADS):
                colsum = jnp.sum(term[:, h * HEAD_W:(h + 1) * HEAD_W], axis=-1, keepdims=True)
                if lo:
                    colsum = jnp.concatenate([jnp.zeros((lo, 1), F32), colsum], axis=0)
                a[h] = jnp.where(col == s, colsum, a[h])
        gate = nw_t * _sigmoid(hg_ref[r, :])
        for h in range(HEADS):
            cs = slice(h * HEAD_W, (h + 1) * HEAD_W)
            i = q * HEADS + h
            a_h = jnp.where(causal, a[h], 0.0).astype(BF16)
            o = (jnp.dot(a_h, v_bf[:, cs], preferred_element_type=F32)
                 + lax.dot_general(qe[:, cs], st_scr[i].astype(BF16), (((1,), (1,)), ((), ())),
                                   preferred_element_type=F32))
            u_t = lax.dot_general(v_bf[:, cs], kd[:, cs], (((0,), (0,)), ((), ())),
                                  preferred_element_type=F32)
            st_scr[i] = st_scr[i] * decay[:, cs] + u_t
            rms = lax.rsqrt(jnp.mean(o * o, axis=-1, keepdims=True) + RMS_EPS)
            o_ref[r, cs] = o * rms * gate[:, cs]

    def chunk(ci, carry):
        for q in range(nbb):
            seq_chunk(pl.ds(pl.multiple_of(q * tb + ci * c, c), c), q)
        return carry

    n_chunks = tb // c
    if n_chunks == 1:
        chunk(0, 0)
    else:
        lax.fori_loop(0, n_chunks, chunk, 0, unroll=2)

    @pl.when(t == pl.num_programs(1) - 1)
    def _():
        for q in range(nbb):
            for h in range(HEADS):
                s_ref[q, h] = st_scr[q * HEADS + h].T


def _hgrn(proj, hg_lb, hg_norm_w, s0, nb, seq, nbb, tb, c):
    nt = seq // tb
    assert nbb == 1 or nt == 1
    has_s0 = s0 is not None

    def tok(colblk):
        return pl.BlockSpec((nbb * tb, BRANCH_W), lambda b, t: (b * nt + t, colblk))

    state_spec = pl.BlockSpec((nbb, HEADS, HEAD_W, HEAD_W), lambda b, t: (b, 0, 0, 0))
    in_specs = [tok(COL_HQ // HEADS), tok(COL_HF // HEADS), tok(COL_HI // HEADS), tok(COL_HG // HEADS),
                pl.BlockSpec(hg_lb.shape, lambda b, t: (0, 0)),
                pl.BlockSpec((1, HEAD_W), lambda b, t: (0, 0))]
    args = [proj, proj, proj, proj, hg_lb, hg_norm_w]
    if has_s0:
        in_specs.append(state_spec)
        args.append(s0)
    return pl.pallas_call(
        functools.partial(_hgrn_kernel, nbb=nbb, tb=tb, c=c, has_s0=has_s0),
        grid=(nb // nbb, nt),
        in_specs=in_specs,
        out_specs=[pl.BlockSpec((nbb * tb, BRANCH_W), lambda b, t: (b * nt + t, 0)), state_spec],
        out_shape=[jax.ShapeDtypeStruct((nb * seq, BRANCH_W), F32),
                   jax.ShapeDtypeStruct((nb, HEADS, HEAD_W, HEAD_W), F32)],
        scratch_shapes=[pltpu.VMEM((nbb * HEADS, HEAD_W, HEAD_W), F32)],
        compiler_params=_cparams(("parallel", "arbitrary")),
        name="hgrn_s0" if has_s0 else "hgrn",
    )(*args)


def _lam(q1_ref, k1_ref, q2_ref, k2_ref):
    s1 = jnp.sum(q1_ref[...] * k1_ref[...], axis=-1, keepdims=True)
    s2 = jnp.sum(q2_ref[...] * k2_ref[...], axis=-1, keepdims=True)
    return jnp.exp(s1) - jnp.exp(s2) + LAM_INIT


def _da_finish(o1, o2, lam, nw):
    o = o1 - lam * o2
    rms = lax.rsqrt(jnp.mean(o * o, axis=-1, keepdims=True) + RMS_EPS)
    return o * rms * nw * (1.0 - LAM_INIT)


def _attn_prompt_kernel(q_ref, k_ref, v_ref, lq1, lk1, lq2, lk2, nw_ref, o_ref, kb, vb, *, tq, tk):
    qi = pl.program_id(2)

    @pl.when(qi == 0)
    def _():
        kb[...] = k_ref[...].astype(BF16)
        vb[...] = v_ref[...].astype(BF16)

    q = q_ref[...] * (DA_HEAD_DIM ** -0.5)
    lane = lax.broadcasted_iota(jnp.int32, q.shape, 1)
    qa = jnp.where(lane < DA_HEAD_DIM, q, 0.0).astype(BF16)
    qb = jnp.where(lane >= DA_HEAD_DIM, q, 0.0).astype(BF16)
    row0 = lax.broadcasted_iota(jnp.int32, (tq, tk), 0)
    col0 = lax.broadcasted_iota(jnp.int32, (tq, tk), 1)

    def body(j, carry, masked=False):
        m1, l1, a1, m2, l2, a2 = carry
        ks = pl.ds(pl.multiple_of(j * tk, tk), tk)
        kj = kb[ks, :]
        vj = vb[ks, :]
        outs = []
        for qm, m, l, a in ((qa, m1, l1, a1), (qb, m2, l2, a2)):
            s = lax.dot_general(qm, kj, (((1,), (1,)), ((), ())), preferred_element_type=F32)
            if masked:
                s = jnp.where(col0 <= row0, s, NEG)
            m_new = jnp.maximum(m, jnp.max(s, axis=-1, keepdims=True))
            corr = jnp.exp(m - m_new)
            p = jnp.exp(s - m_new)
            l_new = l * corr + jnp.sum(p, axis=-1, keepdims=True)
            a_new = a * corr + jnp.dot(p.astype(BF16), vj, preferred_element_type=F32)
            outs += [m_new, l_new, a_new]
        return tuple(outs)

    init = (jnp.full((tq, 1), NEG, F32), jnp.zeros((tq, 1), F32), jnp.zeros((tq, HEAD_W), F32)) * 2
    carry = lax.fori_loop(0, qi, body, init)
    m1, l1, a1, m2, l2, a2 = body(qi, carry, masked=True)
    lam = _lam(lq1, lk1, lq2, lk2)
    o_ref[...] = _da_finish(a1 / l1, a2 / l2, lam, nw_ref[...])


def _attn_prompt(proj, lam_q1, lam_k1, lam_q2, lam_k2, da_norm_w, nb, seq, tq, tk):
    nq = seq // tq
    small = lambda shp: pl.BlockSpec(shp, lambda b, h, i: (0, 0))
    return pl.pallas_call(
        functools.partial(_attn_prompt_kernel, tq=tq, tk=tk),
        grid=(nb, HEADS, nq),
        in_specs=[pl.BlockSpec((tq, HEAD_W), lambda b, h, i: (b * nq + i, COL_Q + h)),
                  pl.BlockSpec((seq, HEAD_W), lambda b, h, i: (b, COL_K + h)),
                  pl.BlockSpec((seq, HEAD_W), lambda b, h, i: (b, COL_V + h)),
                  small((1, DA_HEAD_DIM)), small((1, DA_HEAD_DIM)),
                  small((1, DA_HEAD_DIM)), small((1, DA_HEAD_DIM)),
                  small((1, HEAD_W))],
        out_specs=pl.BlockSpec((tq, HEAD_W), lambda b, h, i: (b * nq + i, h)),
        out_shape=jax.ShapeDtypeStruct((nb * seq, BRANCH_W), F32),
        scratch_shapes=[pltpu.VMEM((seq, HEAD_W), BF16), pltpu.VMEM((seq, HEAD_W), BF16)],
        compiler_params=_cparams(("parallel", "parallel", "arbitrary")),
        name="attn_prompt",
    )(proj, proj, proj, lam_q1, lam_k1, lam_q2, lam_k2, da_norm_w)


def _attn_sample_kernel(pt_ref, q_ref, kn_ref, vn_ref, *rest, pp, tnew):
    k_refs = rest[:pp]
    v_refs = rest[pp:2 * pp]
    lq1, lk1, lq2, lk2, nw_ref, o_ref, qblk, kcat, vcat, m_scr, l_scr, acc = rest[2 * pp:]
    g = pl.program_id(1)
    rows = HEADS * 2 * tnew

    @pl.when(g == 0)
    def _():
        q = q_ref[...] * (DA_HEAD_DIM ** -0.5)
        tiled = jnp.concatenate([q] * (HEADS * 2), axis=0)
        r = lax.broadcasted_iota(jnp.int32, tiled.shape, 0)
        cidx = lax.broadcasted_iota(jnp.int32, tiled.shape, 1)
        qblk[...] = jnp.where((r // tnew) == (cidx // DA_HEAD_DIM), tiled, 0.0).astype(BF16)
        m_scr[...] = jnp.full_like(m_scr, NEG)
        l_scr[...] = jnp.zeros_like(l_scr)
        acc[...] = jnp.zeros_like(acc)

    def update(s, kv_len):
        m_old = m_scr[...]
        m_new = jnp.maximum(m_old, jnp.max(s, axis=-1, keepdims=True))
        corr = jnp.exp(m_old - m_new)
        p = jnp.exp(s - m_new)
        l_scr[...] = l_scr[...] * corr + jnp.sum(p, axis=-1, keepdims=True)
        m_scr[...] = m_new
        p_bf = p.astype(BF16)
        for h in range(HEADS):
            rs = slice(h * 2 * tnew, (h + 1) * 2 * tnew)
            pv = jnp.dot(p_bf[rs, :], vcat[h, 0:kv_len, :], preferred_element_type=F32)
            acc[rs, :] = acc[rs, :] * corr[rs, :] + pv

    for j in range(pp):
        ls = slice(j * PAGE, (j + 1) * PAGE)
        kcat[:, ls] = k_refs[j][...].astype(BF16)
        for h in range(HEADS):
            vcat[h, ls, :] = v_refs[j][pl.ds(h, PAGE, stride=HEADS), :].astype(BF16)
    s = jnp.dot(qblk[...], kcat[...], preferred_element_type=F32)
    update(s, pp * PAGE)

    @pl.when(g == pl.num_programs(1) - 1)
    def _():
        pad = jnp.zeros((PAGE - tnew, BRANCH_W), F32)
        kn = jnp.concatenate([kn_ref[...], pad], axis=0).astype(BF16)
        vn = jnp.concatenate([vn_ref[...], pad], axis=0).astype(BF16)
        for h in range(HEADS):
            vcat[h, 0:PAGE, :] = vn[:, h * HEAD_W:(h + 1) * HEAD_W]
        sn = lax.dot_general(qblk[...], kn, (((1,), (1,)), ((), ())),
                             preferred_element_type=F32)
        r = lax.broadcasted_iota(jnp.int32, sn.shape, 0)
        cidx = lax.broadcasted_iota(jnp.int32, sn.shape, 1)
        sn = jnp.where(cidx <= (r % tnew), sn, NEG)
        update(sn, PAGE)
        lam = _lam(lq1, lk1, lq2, lk2)
        o = acc[...] / l_scr[...]
        nw = nw_ref[...]
        for h in range(HEADS):
            o1 = o[h * 2 * tnew:h * 2 * tnew + tnew, :]
            o2 = o[h * 2 * tnew + tnew:(h + 1) * 2 * tnew, :]
            o_ref[:, h * HEAD_W:(h + 1) * HEAD_W] = _da_finish(o1, o2, lam, nw)


def _attn_sample(proj, cache_k, cache_v, page_table, lam_q1, lam_k1, lam_q2, lam_k2, da_norm_w,
                 nb, tnew, pp):
    n_pages = page_table.shape[1]
    ck = cache_k.reshape(-1, PAGE, BRANCH_W).transpose(0, 2, 1)
    cv = cache_v.reshape(-1, PAGE * HEADS, HEAD_W)
    rows = HEADS * 2 * tnew

    def page_spec(j):
        return pl.BlockSpec((None, BRANCH_W, PAGE), lambda b, g, pt: (pt[b, g * pp + j], 0, 0))

    small = lambda shp: pl.BlockSpec(shp, lambda b, g, pt: (0, 0))
    grid_spec = pltpu.PrefetchScalarGridSpec(
        num_scalar_prefetch=1,
        grid=(nb, n_pages // pp),
        in_specs=[pl.BlockSpec((tnew, BRANCH_W), lambda b, g, pt: (b, 0)),
                  pl.BlockSpec((tnew, BRANCH_W), lambda b, g, pt: (b, 1)),
                  pl.BlockSpec((tnew, BRANCH_W), lambda b, g, pt: (b, 2))]
                 + [page_spec(j) for j in range(pp)] + [page_spec(j) for j in range(pp)]
                 + [small((1, DA_HEAD_DIM))] * 4 + [small((1, HEAD_W))],
        out_specs=pl.BlockSpec((tnew, BRANCH_W), lambda b, g, pt: (b, 0)),
        scratch_shapes=[pltpu.VMEM((rows, BRANCH_W), BF16),
                        pltpu.VMEM((BRANCH_W, pp * PAGE), BF16),
                        pltpu.VMEM((HEADS, pp * PAGE, HEAD_W), BF16),
                        pltpu.VMEM((rows, 1), F32),
                        pltpu.VMEM((rows, 1), F32),
                        pltpu.VMEM((rows, HEAD_W), F32)],
    )
    return pl.pallas_call(
        functools.partial(_attn_sample_kernel, pp=pp, tnew=tnew),
        grid_spec=grid_spec,
        out_shape=jax.ShapeDtypeStruct((nb * tnew, BRANCH_W), F32),
        compiler_params=_cparams(("parallel", "arbitrary")),
        name="attn_sample",
    )(page_table, proj, proj, proj, *([ck] * pp), *([cv] * pp),
      lam_q1, lam_k1, lam_q2, lam_k2, da_norm_w)


def _layernorm(x, w, b):
    mu = jnp.mean(x, axis=-1, keepdims=True)
    xc = x - mu
    var = jnp.mean(xc * xc, axis=-1, keepdims=True)
    return xc * lax.rsqrt(var + LN_EPS) * w + b


def _merge_kernel(x_ref, ohg_ref, oda_ref, ga_ref, gb_ref, wpa_ref, wpb_ref, wout_ref,
                  lw_ref, lb_ref, h_ref):
    pa = jnp.dot(ohg_ref[...].astype(BF16), wpa_ref[...], preferred_element_type=F32)
    pb = jnp.dot(oda_ref[...].astype(BF16), wpb_ref[...], preferred_element_type=F32)
    merged = _sigmoid(ga_ref[...]) * pa + _sigmoid(gb_ref[...]) * pb
    y = ALPHA * x_ref[...] + jnp.dot(merged.astype(BF16), wout_ref[...], preferred_element_type=F32)
    h_ref[...] = _layernorm(y, lw_ref[...], lb_ref[...])


def _merge(x, o_hg, o_da, gates, wpa, wpb, wout, ln_w, ln_b, tm):
    t = x.shape[0]
    rowblk = lambda w: pl.BlockSpec((tm, w), lambda i: (i, 0))
    const = lambda shp: pl.BlockSpec(shp, lambda i: (0, 0), pipeline_mode=pl.Buffered(1))
    return pl.pallas_call(
        _merge_kernel,
        grid=(t // tm,),
        in_specs=[rowblk(D_MODEL), rowblk(BRANCH_W), rowblk(BRANCH_W),
                  pl.BlockSpec((tm, D_MODEL), lambda i: (i, 0)),
                  pl.BlockSpec((tm, D_MODEL), lambda i: (i, 1)),
                  const((BRANCH_W, D_MODEL)), const((BRANCH_W, D_MODEL)), const((D_MODEL, D_MODEL)),
                  const((1, D_MODEL)), const((1, D_MODEL))],
        out_specs=rowblk(D_MODEL),
        out_shape=jax.ShapeDtypeStruct((t, D_MODEL), F32),
        compiler_params=_cparams(("parallel",)),
        name="merge_out_ln",
    )(x, o_hg, o_da, gates, gates, wpa, wpb, wout, ln_w, ln_b)


def _mlp_kernel(h_ref, wup_ref, wdn_ref, lw_ref, lb_ref, y_ref, hb):
    j = pl.program_id(1)

    @pl.when(j == 0)
    def _():
        hb[...] = h_ref[...].astype(BF16)
        y_ref[...] = jnp.zeros_like(y_ref)

    u = jnp.dot(hb[...], wup_ref[...], preferred_element_type=F32)
    u = jnp.maximum(u, 0.0)
    y_ref[...] += jnp.dot((u * u).astype(BF16), wdn_ref[...], preferred_element_type=F32)

    @pl.when(j == pl.num_programs(1) - 1)
    def _():
        y_ref[...] = _layernorm(ALPHA * h_ref[...] + y_ref[...], lw_ref[...], lb_ref[...])


def _mlp(h, wup, wdn, ln_w, ln_b, tm, tf):
    t = h.shape[0]
    return pl.pallas_call(
        _mlp_kernel,
        grid=(t // tm, D_FF // tf),
        in_specs=[pl.BlockSpec((tm, D_MODEL), lambda i, j: (i, 0), pipeline_mode=pl.Buffered(1)),
                  pl.BlockSpec((D_MODEL, tf), lambda i, j: (0, j)),
                  pl.BlockSpec((tf, D_MODEL), lambda i, j: (j, 0)),
                  pl.BlockSpec((1, D_MODEL), lambda i, j: (0, 0)),
                  pl.BlockSpec((1, D_MODEL), lambda i, j: (0, 0))],
        out_specs=pl.BlockSpec((tm, D_MODEL), lambda i, j: (i, 0)),
        out_shape=jax.ShapeDtypeStruct((t, D_MODEL), F32),
        scratch_shapes=[pltpu.VMEM((tm, D_MODEL), BF16)],
        compiler_params=_cparams(("parallel", "arbitrary")),
        name="mlp_ln",
    )(h, wup, wdn, ln_w, ln_b)


def kernel(x_prompt, x_sample, cache_k, cache_v, state_hgrn, page_table, w_in, hg_lb, hg_norm_w,
           lam_q1, lam_k1, lam_q2, lam_k2, da_norm_w, w_pa, w_pb, w_out, ln1_w, ln1_b,
           w_up, w_down, ln2_w, ln2_b):
    nb, seq, _ = x_prompt.shape
    db, tnew, _ = x_sample.shape
    w_main = w_in[0, :, :N_MAIN].astype(BF16)
    w_gate = w_in[0, :, N_MAIN:].astype(BF16)
    wpa = w_pa[0].astype(BF16)
    wpb = w_pb[0].astype(BF16)
    wout = w_out[0].astype(BF16)
    wup = w_up[0].astype(BF16)
    wdn = w_down[0].astype(BF16)

    def trunk(x2d, tm_proj, tm_merge, tm_mlp, attend, hgrn):
        x_bf = x2d.astype(BF16)
        proj = _in_proj(x_bf, w_main, tm_proj, 1024)
        gates = _in_proj(x_bf, w_gate, tm_proj, 1024)
        o_hg, s_new = hgrn(proj)
        o_da = attend(proj)
        h = _merge(x2d, o_hg, o_da, gates, wpa, wpb, wout, ln1_w, ln1_b, tm_merge)
        y = _mlp(h, wup, wdn, ln2_w, ln2_b, tm_mlp, 512)
        return y, proj, s_new

    lam_args = (lam_q1, lam_k1, lam_q2, lam_k2, da_norm_w)

    y_p, proj_p, s_p = trunk(
        x_prompt.reshape(nb * seq, D_MODEL), 1024, 256, 1024,
        lambda pr: _attn_prompt(pr, *lam_args, nb, seq, 512, 512),
        lambda pr: _hgrn(pr, hg_lb, hg_norm_w, None, nb, seq, 1, 256, 16))
    y_s, proj_s, s_s = trunk(
        x_sample.reshape(db * tnew, D_MODEL), 256, 256, 256,
        lambda pr: _attn_sample(pr, cache_k, cache_v, page_table, *lam_args, db, tnew, 8),
        lambda pr: _hgrn(pr, hg_lb, hg_norm_w, state_hgrn.reshape(db, HEADS, HEAD_W, HEAD_W),
                         db, tnew, 4, tnew, tnew))

    def kv(proj, b, l):
        k = proj[:, BRANCH_W:2 * BRANCH_W].reshape(1, b, l, HEADS, 2, DA_HEAD_DIM)
        v = proj[:, 2 * BRANCH_W:3 * BRANCH_W].reshape(1, b, l, HEADS, HEAD_W)
        return k, v

    k_p, v_p = kv(proj_p, nb, seq)
    k_s, v_s = kv(proj_s, db, tnew)
    return (y_p.reshape(nb, seq, D_MODEL), y_s.reshape(db, tnew, D_MODEL),
            k_p, v_p, s_p[None], k_s, v_s, s_s[None])
```

```python
import functools
import math

import jax
import jax.numpy as jnp
from jax import lax
from jax.experimental import pallas as pl
from jax.experimental.pallas import tpu as pltpu

D_MODEL = 2048
HEADS = 8
HEAD_W = 128
DA_HEAD_DIM = 64
BRANCH_W = HEADS * HEAD_W
D_FF = 4 * D_MODEL
PAGE = 128
LN_EPS = 1e-5
RMS_EPS = 1e-6
DEPTH = 1
ALPHA = (2 * DEPTH) ** 0.25
LAM_INIT = 0.8 - 0.6 * math.exp(-0.3 * 0)
IN_WIDTH = 7 * BRANCH_W + 2 * D_MODEL
BLK_Q, BLK_K, BLK_V, BLK_HQ, BLK_GATES = 0, 1, 2, 3, 7
NEG = -1e30

F32 = jnp.float32
BF16 = jnp.bfloat16
VMEM_LIMIT = 52 * 1024 * 1024


def _cparams(sem):
    return pltpu.CompilerParams(dimension_semantics=sem, vmem_limit_bytes=VMEM_LIMIT)


def _sigmoid(x):
    return 1.0 / (1.0 + jnp.exp(-x))


def _proj_kernel(x_ref, w_ref, o_ref, wb):
    @pl.when(pl.program_id(1) == 0)
    def _():
        wb[...] = w_ref[...].astype(BF16)

    o_ref[...] = jnp.dot(x_ref[...], wb[...], preferred_element_type=F32)


def _in_proj(x_bf, w_in, tm, skip_kv):
    t, k = x_bf.shape
    tn = BRANCH_W
    n_blk = IN_WIDTH // tn - (2 if skip_kv else 0)
    wcol = (lambda j: j + 2 * jnp.minimum(j, 1)) if skip_kv else (lambda j: j)
    return pl.pallas_call(
        _proj_kernel,
        grid=(n_blk, t // tm),
        in_specs=[pl.BlockSpec((tm, k), lambda j, i: (i, 0)),
                  pl.BlockSpec((None, k, tn), lambda j, i: (0, 0, wcol(j)))],
        out_specs=pl.BlockSpec((tm, tn), lambda j, i: (i, j)),
        out_shape=jax.ShapeDtypeStruct((t, n_blk * tn), F32),
        scratch_shapes=[pltpu.VMEM((k, tn), BF16)],
        compiler_params=_cparams(("parallel", "arbitrary")),
        name="in_proj",
    )(x_bf, w_in)


def _proj_kv_kernel(x_ref, wk_ref, wv_ref, v_ref, kt_ref, vt_ref, wkb, wvb):
    @pl.when(pl.program_id(0) == 0)
    def _():
        wkb[...] = wk_ref[...].astype(BF16)
        wvb[...] = wv_ref[...].astype(BF16)

    x = x_ref[...]
    tm = x.shape[0]
    kt_ref[...] = jnp.dot(x, wkb[...], preferred_element_type=F32).T
    v = jnp.dot(x, wvb[...], preferred_element_type=F32)
    v_ref[...] = v
    for h in range(HEADS):
        vt_ref[pl.ds(h, tm, stride=HEADS), :] = v[:, h * HEAD_W:(h + 1) * HEAD_W]


def _proj_kv(x_bf, w_in, nb, seq, tm):
    t, k = x_bf.shape
    nt = seq // tm
    wspec = lambda c: pl.BlockSpec((None, k, BRANCH_W), lambda i: (0, 0, c), pipeline_mode=pl.Buffered(1))
    return pl.pallas_call(
        _proj_kv_kernel,
        grid=(t // tm,),
        in_specs=[pl.BlockSpec((tm, k), lambda i: (i, 0)), wspec(1), wspec(2)],
        out_specs=[pl.BlockSpec((tm, BRANCH_W), lambda i: (i, 0)),
                   pl.BlockSpec((None, BRANCH_W, tm), lambda i: (i // nt, 0, i % nt)),
                   pl.BlockSpec((tm * HEADS, HEAD_W), lambda i: (i, 0))],
        out_shape=[jax.ShapeDtypeStruct((t, BRANCH_W), F32),
                   jax.ShapeDtypeStruct((nb, BRANCH_W, seq), F32),
                   jax.ShapeDtypeStruct((t * HEADS, HEAD_W), F32)],
        scratch_shapes=[pltpu.VMEM((k, BRANCH_W), BF16), pltpu.VMEM((k, BRANCH_W), BF16)],
        compiler_params=_cparams(("arbitrary",)),
        name="proj_kv",
    )(x_bf, w_in, w_in)


def _hgrn_kernel(*refs, nbb, tb, c, has_s0):
    if has_s0:
        hq_ref, hf_ref, hi_ref, hg_ref, lb_ref, nw_ref, s0_ref, o_ref, s_ref, st_scr = refs
    else:
        hq_ref, hf_ref, hi_ref, hg_ref, lb_ref, nw_ref, o_ref, s_ref, st_scr = refs
        s0_ref = None
    t = pl.program_id(1)

    @pl.when(t == 0)
    def _():
        for q in range(nbb):
            for h in range(HEADS):
                if has_s0:
                    st_scr[q * HEADS + h] = s0_ref[q, h].T
                else:
                    st_scr[q * HEADS + h] = jnp.zeros((HEAD_W, HEAD_W), F32)

    lbx = lb_ref[...]
    lbe = jnp.exp(lbx - jnp.max(lbx, axis=0, keepdims=True))
    lb_all = lbe[0:1, :] / jnp.sum(lbe, axis=0, keepdims=True)
    nw_t = jnp.concatenate([nw_ref[...]] * HEADS, axis=1)

    row = lax.broadcasted_iota(jnp.int32, (c, c), 0)
    col = lax.broadcasted_iota(jnp.int32, (c, c), 1)
    tri = jnp.where(row >= col, 1.0, 0.0).astype(BF16)
    causal = row >= col

    one_m_lb = 1.0 - lb_all

    def seq_chunk(r, q):
        z = hf_ref[r, :]
        logf = jnp.log(lb_all + one_m_lb * _sigmoid(z))
        kk = one_m_lb * _sigmoid(-z)
        hq = hq_ref[r, :]
        qq = hq * _sigmoid(hq)
        v_bf = hi_ref[r, :].astype(BF16)
        p0 = logf.astype(BF16)
        r1 = logf - p0.astype(F32)
        p1 = r1.astype(BF16)
        p2 = (r1 - p1.astype(F32)).astype(BF16)
        b = (jnp.dot(tri, p0, preferred_element_type=F32)
             + jnp.dot(tri, p1, preferred_element_type=F32)
             + jnp.dot(tri, p2, preferred_element_type=F32))
        b_last = b[c - 1:c, :]
        decay = jnp.exp(b_last)
        qe = (qq * jnp.exp(b)).astype(BF16)
        kd = (kk * jnp.exp(b_last - b)).astype(BF16)
        a = [jnp.zeros((c, c), F32) for _ in range(HEADS)]
        for s in range(c):
            lo = (s // 8) * 8
            d = jnp.minimum(b[lo:, :] - b[s:s + 1, :], 0.0)
            term = qq[lo:, :] * kk[s:s + 1, :] * jnp.exp(d)
            for h in range(HEADS):
                colsum = jnp.sum(term[:, h * HEAD_W:(h + 1) * HEAD_W], axis=-1, keepdims=True)
                if lo:
                    colsum = jnp.concatenate([jnp.zeros((lo, 1), F32), colsum], axis=0)
                a[h] = jnp.where(col == s, colsum, a[h])
        gate = nw_t * _sigmoid(hg_ref[r, :])
        for h in range(HEADS):
            cs = slice(h * HEAD_W, (h + 1) * HEAD_W)
            i = q * HEADS + h
            a_h = jnp.where(causal, a[h], 0.0).astype(BF16)
            o = (jnp.dot(a_h, v_bf[:, cs], preferred_element_type=F32)
                 + lax.dot_general(qe[:, cs], st_scr[i].astype(BF16), (((1,), (1,)), ((), ())),
                                   preferred_element_type=F32))
            u_t = lax.dot_general(v_bf[:, cs], kd[:, cs], (((0,), (0,)), ((), ())),
                                  preferred_element_type=F32)
            st_scr[i] = st_scr[i] * decay[:, cs] + u_t
            rms = lax.rsqrt(jnp.mean(o * o, axis=-1, keepdims=True) + RMS_EPS)
            o_ref[r, cs] = o * rms * gate[:, cs]

    def chunk(ci, carry):
        for q in range(nbb):
            seq_chunk(pl.ds(pl.multiple_of(q * tb + ci * c, c), c), q)
        return carry

    n_chunks = tb // c
    if n_chunks == 1:
        chunk(0, 0)
    else:
        lax.fori_loop(0, n_chunks, chunk, 0, unroll=2)

    @pl.when(t == pl.num_programs(1) - 1)
    def _():
        for q in range(nbb):
            for h in range(HEADS):
                s_ref[q, h] = st_scr[q * HEADS + h].T


def _hgrn(proj, blk_hq, hg_lb, hg_norm_w, s0, nb, seq, nbb, tb, c):
    nt = seq // tb
    assert nbb == 1 or nt == 1
    has_s0 = s0 is not None

    def tok(colblk):
        return pl.BlockSpec((nbb * tb, BRANCH_W), lambda b, t: (b * nt + t, colblk))

    state_spec = pl.BlockSpec((nbb, HEADS, HEAD_W, HEAD_W), lambda b, t: (b, 0, 0, 0))
    in_specs = [tok(blk_hq), tok(blk_hq + 1), tok(blk_hq + 2), tok(blk_hq + 3),
                pl.BlockSpec(hg_lb.shape, lambda b, t: (0, 0)),
                pl.BlockSpec((1, HEAD_W), lambda b, t: (0, 0))]
    args = [proj, proj, proj, proj, hg_lb, hg_norm_w]
    if has_s0:
        in_specs.append(state_spec)
        args.append(s0)
    return pl.pallas_call(
        functools.partial(_hgrn_kernel, nbb=nbb, tb=tb, c=c, has_s0=has_s0),
        grid=(nb // nbb, nt),
        in_specs=in_specs,
        out_specs=[pl.BlockSpec((nbb * tb, BRANCH_W), lambda b, t: (b * nt + t, 0)), state_spec],
        out_shape=[jax.ShapeDtypeStruct((nb * seq, BRANCH_W), F32),
                   jax.ShapeDtypeStruct((nb, HEADS, HEAD_W, HEAD_W), F32)],
        scratch_shapes=[pltpu.VMEM((nbb * HEADS, HEAD_W, HEAD_W), F32)],
        compiler_params=_cparams(("parallel", "arbitrary")),
        name="hgrn_s0" if has_s0 else "hgrn",
    )(*args)


def _lam(q1_ref, k1_ref, q2_ref, k2_ref):
    s1 = jnp.sum(q1_ref[...] * k1_ref[...], axis=-1, keepdims=True)
    s2 = jnp.sum(q2_ref[...] * k2_ref[...], axis=-1, keepdims=True)
    return jnp.exp(s1) - jnp.exp(s2) + LAM_INIT


def _da_finish(o1, o2, lam, nw):
    o = o1 - lam * o2
    rms = lax.rsqrt(jnp.mean(o * o, axis=-1, keepdims=True) + RMS_EPS)
    return o * rms * nw * (1.0 - LAM_INIT)


def _attn_prompt_kernel(q_ref, k_ref, v_ref, lq1, lk1, lq2, lk2, nw_ref, o_ref, kb, vb, *, tq, tk):
    qi = pl.program_id(2)

    @pl.when(qi == 0)
    def _():
        kb[...] = k_ref[...].astype(BF16)
        vb[...] = v_ref[...].astype(BF16)

    q = q_ref[...] * (DA_HEAD_DIM ** -0.5)
    lane = lax.broadcasted_iota(jnp.int32, q.shape, 1)
    qa = jnp.where(lane < DA_HEAD_DIM, q, 0.0).astype(BF16)
    qb = jnp.where(lane >= DA_HEAD_DIM, q, 0.0).astype(BF16)
    row0 = lax.broadcasted_iota(jnp.int32, (tq, tk), 0)
    col0 = lax.broadcasted_iota(jnp.int32, (tq, tk), 1)

    def body(j, carry, masked=False):
        m1, l1, a1, m2, l2, a2 = carry
        ks = pl.ds(pl.multiple_of(j * tk, tk), tk)
        kj = kb[:, ks]
        vj = vb[ks, :]
        outs = []
        for qm, m, l, a in ((qa, m1, l1, a1), (qb, m2, l2, a2)):
            s = jnp.dot(qm, kj, preferred_element_type=F32)
            if masked:
                s = jnp.where(col0 <= row0, s, NEG)
            m_new = jnp.maximum(m, jnp.max(s, axis=-1, keepdims=True))
            corr = jnp.exp(m - m_new)
            p = jnp.exp(s - m_new)
            l_new = l * corr + jnp.sum(p, axis=-1, keepdims=True)
            a_new = a * corr + jnp.dot(p.astype(BF16), vj, preferred_element_type=F32)
            outs += [m_new, l_new, a_new]
        return tuple(outs)

    init = (jnp.full((tq, 1), NEG, F32), jnp.zeros((tq, 1), F32), jnp.zeros((tq, HEAD_W), F32)) * 2
    carry = lax.fori_loop(0, qi, body, init)
    m1, l1, a1, m2, l2, a2 = body(qi, carry, masked=True)
    lam = _lam(lq1, lk1, lq2, lk2)
    o_ref[...] = _da_finish(a1 / l1, a2 / l2, lam, nw_ref[...])


def _attn_prompt(proj, kt, v, lam_q1, lam_k1, lam_q2, lam_k2, da_norm_w, nb, seq, tq, tk):
    assert tq == tk
    nq = seq // tq
    small = lambda shp: pl.BlockSpec(shp, lambda b, h, i: (0, 0))
    return pl.pallas_call(
        functools.partial(_attn_prompt_kernel, tq=tq, tk=tk),
        grid=(nb, HEADS, nq),
        in_specs=[pl.BlockSpec((tq, HEAD_W), lambda b, h, i: (b * nq + i, BLK_Q * HEADS + h)),
                  pl.BlockSpec((None, HEAD_W, seq), lambda b, h, i: (b, h, 0)),
                  pl.BlockSpec((seq, HEAD_W), lambda b, h, i: (b, h)),
                  small((1, DA_HEAD_DIM)), small((1, DA_HEAD_DIM)),
                  small((1, DA_HEAD_DIM)), small((1, DA_HEAD_DIM)),
                  small((1, HEAD_W))],
        out_specs=pl.BlockSpec((tq, HEAD_W), lambda b, h, i: (b * nq + i, h)),
        out_shape=jax.ShapeDtypeStruct((nb * seq, BRANCH_W), F32),
        scratch_shapes=[pltpu.VMEM((HEAD_W, seq), BF16), pltpu.VMEM((seq, HEAD_W), BF16)],
        compiler_params=_cparams(("parallel", "parallel", "arbitrary")),
        name="attn_prompt",
    )(proj, kt, v, lam_q1, lam_k1, lam_q2, lam_k2, da_norm_w)


def _attn_sample_kernel(pt_ref, q_ref, kn_ref, vn_ref, *rest, pp, tnew):
    k_refs = rest[:pp]
    v_refs = rest[pp:2 * pp]
    lq1, lk1, lq2, lk2, nw_ref, o_ref, qblk, kcat, vcat, m_scr, l_scr, acc = rest[2 * pp:]
    g = pl.program_id(1)
    rows = HEADS * 2 * tnew

    @pl.when(g == 0)
    def _():
        q = q_ref[...] * (DA_HEAD_DIM ** -0.5)
        tiled = jnp.concatenate([q] * (HEADS * 2), axis=0)
        r = lax.broadcasted_iota(jnp.int32, tiled.shape, 0)
        cidx = lax.broadcasted_iota(jnp.int32, tiled.shape, 1)
        qblk[...] = jnp.where((r // tnew) == (cidx // DA_HEAD_DIM), tiled, 0.0).astype(BF16)
        m_scr[...] = jnp.full_like(m_scr, NEG)
        l_scr[...] = jnp.zeros_like(l_scr)
        acc[...] = jnp.zeros_like(acc)

    def update(s, kv_len):
        m_old = m_scr[...]
        m_new = jnp.maximum(m_old, jnp.max(s, axis=-1, keepdims=True))
        corr = jnp.exp(m_old - m_new)
        p = jnp.exp(s - m_new)
        l_scr[...] = l_scr[...] * corr + jnp.sum(p, axis=-1, keepdims=True)
        m_scr[...] = m_new
        p_bf = p.astype(BF16)
        for h in range(HEADS):
            rs = slice(h * 2 * tnew, (h + 1) * 2 * tnew)
            pv = jnp.dot(p_bf[rs, :], vcat[h, 0:kv_len, :], preferred_element_type=F32)
            acc[rs, :] = acc[rs, :] * corr[rs, :] + pv

    for j in range(pp):
        ls = slice(j * PAGE, (j + 1) * PAGE)
        kcat[:, ls] = k_refs[j][...].astype(BF16)
        for h in range(HEADS):
            vcat[h, ls, :] = v_refs[j][pl.ds(h, PAGE, stride=HEADS), :].astype(BF16)
    s = jnp.dot(qblk[...], kcat[...], preferred_element_type=F32)
    update(s, pp * PAGE)

    @pl.when(g == pl.num_programs(1) - 1)
    def _():
        pad = jnp.zeros((PAGE - tnew, BRANCH_W), F32)
        kn = jnp.concatenate([kn_ref[...], pad], axis=0).astype(BF16)
        vn = jnp.concatenate([vn_ref[...], pad], axis=0).astype(BF16)
        for h in range(HEADS):
            vcat[h, 0:PAGE, :] = vn[:, h * HEAD_W:(h + 1) * HEAD_W]
        sn = lax.dot_general(qblk[...], kn, (((1,), (1,)), ((), ())),
                             preferred_element_type=F32)
        r = lax.broadcasted_iota(jnp.int32, sn.shape, 0)
        cidx = lax.broadcasted_iota(jnp.int32, sn.shape, 1)
        sn = jnp.where(cidx <= (r % tnew), sn, NEG)
        update(sn, PAGE)
        lam = _lam(lq1, lk1, lq2, lk2)
        o = acc[...] / l_scr[...]
        nw = nw_ref[...]
        for h in range(HEADS):
            o1 = o[h * 2 * tnew:h * 2 * tnew + tnew, :]
            o2 = o[h * 2 * tnew + tnew:(h + 1) * 2 * tnew, :]
            o_ref[:, h * HEAD_W:(h + 1) * HEAD_W] = _da_finish(o1, o2, lam, nw)


def _attn_sample(proj, cache_k, cache_v, page_table, lam_q1, lam_k1, lam_q2, lam_k2, da_norm_w,
                 nb, tnew, pp):
    n_pages = page_table.shape[1]
    ck = cache_k.reshape(-1, PAGE, BRANCH_W).transpose(0, 2, 1)
    cv = cache_v.reshape(-1, PAGE * HEADS, HEAD_W)
    rows = HEADS * 2 * tnew

    def page_spec(j):
        return pl.BlockSpec((None, BRANCH_W, PAGE), lambda b, g, pt: (pt[b, g * pp + j], 0, 0))

    small = lambda shp: pl.BlockSpec(shp, lambda b, g, pt: (0, 0))
    grid_spec = pltpu.PrefetchScalarGridSpec(
        num_scalar_prefetch=1,
        grid=(nb, n_pages // pp),
        in_specs=[pl.BlockSpec((tnew, BRANCH_W), lambda b, g, pt: (b, 0)),
                  pl.BlockSpec((tnew, BRANCH_W), lambda b, g, pt: (b, 1)),
                  pl.BlockSpec((tnew, BRANCH_W), lambda b, g, pt: (b, 2))]
                 + [page_spec(j) for j in range(pp)] + [page_spec(j) for j in range(pp)]
                 + [small((1, DA_HEAD_DIM))] * 4 + [small((1, HEAD_W))],
        out_specs=pl.BlockSpec((tnew, BRANCH_W), lambda b, g, pt: (b, 0)),
        scratch_shapes=[pltpu.VMEM((rows, BRANCH_W), BF16),
                        pltpu.VMEM((BRANCH_W, pp * PAGE), BF16),
                        pltpu.VMEM((HEADS, pp * PAGE, HEAD_W), BF16),
                        pltpu.VMEM((rows, 1), F32),
                        pltpu.VMEM((rows, 1), F32),
                        pltpu.VMEM((rows, HEAD_W), F32)],
    )
    return pl.pallas_call(
        functools.partial(_attn_sample_kernel, pp=pp, tnew=tnew),
        grid_spec=grid_spec,
        out_shape=jax.ShapeDtypeStruct((nb * tnew, BRANCH_W), F32),
        compiler_params=_cparams(("parallel", "arbitrary")),
        name="attn_sample",
    )(page_table, proj, proj, proj, *([ck] * pp), *([cv] * pp),
      lam_q1, lam_k1, lam_q2, lam_k2, da_norm_w)


def _layernorm(x, w, b):
    mu = jnp.mean(x, axis=-1, keepdims=True)
    xc = x - mu
    var = jnp.mean(xc * xc, axis=-1, keepdims=True)
    return xc * lax.rsqrt(var + LN_EPS) * w + b


def _merge_kernel(x_ref, ohg_ref, oda_ref, ga0_ref, ga1_ref, gb0_ref, gb1_ref, wpa_ref, wpb_ref,
                  wout_ref, lw_ref, lb_ref, h_ref):
    pa = jnp.dot(ohg_ref[...].astype(BF16), wpa_ref[...], preferred_element_type=F32)
    pb = jnp.dot(oda_ref[...].astype(BF16), wpb_ref[...], preferred_element_type=F32)
    half = BRANCH_W
    merged = jnp.concatenate(
        [_sigmoid(ga0_ref[...]) * pa[:, :half] + _sigmoid(gb0_ref[...]) * pb[:, :half],
         _sigmoid(ga1_ref[...]) * pa[:, half:] + _sigmoid(gb1_ref[...]) * pb[:, half:]], axis=1)
    y = ALPHA * x_ref[...] + jnp.dot(merged.astype(BF16), wout_ref[...], preferred_element_type=F32)
    h_ref[...] = _layernorm(y, lw_ref[...], lb_ref[...])


def _merge(x, o_hg, o_da, proj, blk_gates, wpa, wpb, wout, ln_w, ln_b, tm):
    t = x.shape[0]
    rowblk = lambda w: pl.BlockSpec((tm, w), lambda i: (i, 0))
    gate = lambda c: pl.BlockSpec((tm, BRANCH_W), lambda i: (i, blk_gates + c))
    const = lambda shp: pl.BlockSpec(shp, lambda i: (0, 0), pipeline_mode=pl.Buffered(1))
    return pl.pallas_call(
        _merge_kernel,
        grid=(t // tm,),
        in_specs=[rowblk(D_MODEL), rowblk(BRANCH_W), rowblk(BRANCH_W),
                  gate(0), gate(1), gate(2), gate(3),
                  const((BRANCH_W, D_MODEL)), const((BRANCH_W, D_MODEL)), const((D_MODEL, D_MODEL)),
                  const((1, D_MODEL)), const((1, D_MODEL))],
        out_specs=rowblk(D_MODEL),
        out_shape=jax.ShapeDtypeStruct((t, D_MODEL), F32),
        compiler_params=_cparams(("parallel",)),
        name="merge_out_ln",
    )(x, o_hg, o_da, proj, proj, proj, proj, wpa, wpb, wout, ln_w, ln_b)


def _mlp_kernel(h_ref, wup_ref, wdn_ref, lw_ref, lb_ref, y_ref, hb):
    j = pl.program_id(1)

    @pl.when(j == 0)
    def _():
        hb[...] = h_ref[...].astype(BF16)
        y_ref[...] = jnp.zeros_like(y_ref)

    u = jnp.dot(hb[...], wup_ref[...], preferred_element_type=F32)
    u = jnp.maximum(u, 0.0)
    y_ref[...] += jnp.dot((u * u).astype(BF16), wdn_ref[...], preferred_element_type=F32)

    @pl.when(j == pl.num_programs(1) - 1)
    def _():
        y_ref[...] = _layernorm(ALPHA * h_ref[...] + y_ref[...], lw_ref[...], lb_ref[...])


def _mlp(h, wup, wdn, ln_w, ln_b, tm, tf):
    t = h.shape[0]
    return pl.pallas_call(
        _mlp_kernel,
        grid=(t // tm, D_FF // tf),
        in_specs=[pl.BlockSpec((tm, D_MODEL), lambda i, j: (i, 0), pipeline_mode=pl.Buffered(1)),
                  pl.BlockSpec((D_MODEL, tf), lambda i, j: (0, j)),
                  pl.BlockSpec((tf, D_MODEL), lambda i, j: (j, 0)),
                  pl.BlockSpec((1, D_MODEL), lambda i, j: (0, 0)),
                  pl.BlockSpec((1, D_MODEL), lambda i, j: (0, 0))],
        out_specs=pl.BlockSpec((tm, D_MODEL), lambda i, j: (i, 0)),
        out_shape=jax.ShapeDtypeStruct((t, D_MODEL), F32),
        scratch_shapes=[pltpu.VMEM((tm, D_MODEL), BF16)],
        compiler_params=_cparams(("parallel", "arbitrary")),
        name="mlp_ln",
    )(h, wup, wdn, ln_w, ln_b)


def kernel(x_prompt, x_sample, cache_k, cache_v, state_hgrn, page_table, w_in, hg_lb, hg_norm_w,
           lam_q1, lam_k1, lam_q2, lam_k2, da_norm_w, w_pa, w_pb, w_out, ln1_w, ln1_b,
           w_up, w_down, ln2_w, ln2_b):
    nb, seq, _ = x_prompt.shape
    db, tnew, _ = x_sample.shape
    wpa = w_pa[0].astype(BF16)
    wpb = w_pb[0].astype(BF16)
    wout = w_out[0].astype(BF16)
    wup = w_up[0].astype(BF16)
    wdn = w_down[0].astype(BF16)
    lam_args = (lam_q1, lam_k1, lam_q2, lam_k2, da_norm_w)

    def tail(x2d, o_hg, o_da, proj, blk_gates, tm_merge, tm_mlp):
        h = _merge(x2d, o_hg, o_da, proj, blk_gates, wpa, wpb, wout, ln1_w, ln1_b, tm_merge)
        return _mlp(h, wup, wdn, ln2_w, ln2_b, tm_mlp, 512)

    xp = x_prompt.reshape(nb * seq, D_MODEL)
    xp_bf = xp.astype(BF16)
    v_p2d, kt_p, vt_p = _proj_kv(xp_bf, w_in, nb, seq, 512)
    proj_p = _in_proj(xp_bf, w_in, 1024, skip_kv=True)
    o_hg_p, s_p = _hgrn(proj_p, BLK_HQ - 2, hg_lb, hg_norm_w, None, nb, seq, 1, 256, 16)
    o_da_p = _attn_prompt(proj_p, kt_p, v_p2d, *lam_args, nb, seq, 512, 512)
    y_p = tail(xp, o_hg_p, o_da_p, proj_p, BLK_GATES - 2, 256, 1024)
    k_p = kt_p.reshape(nb, HEADS, 2, DA_HEAD_DIM, seq).transpose(0, 4, 1, 2, 3)[None]
    v_p = vt_p.reshape(1, nb, seq, HEADS, HEAD_W)

    xs = x_sample.reshape(db * tnew, D_MODEL)
    proj_s = _in_proj(xs.astype(BF16), w_in, db * tnew, skip_kv=False)
    o_hg_s, s_s = _hgrn(proj_s, BLK_HQ, hg_lb, hg_norm_w,
                        state_hgrn.reshape(db, HEADS, HEAD_W, HEAD_W), db, tnew, 4, tnew, tnew)
    o_da_s = _attn_sample(proj_s, cache_k, cache_v, page_table, *lam_args, db, tnew, 16)
    y_s = tail(xs, o_hg_s, o_da_s, proj_s, BLK_GATES, 256, 256)
    k_s = proj_s[:, BLK_K * BRANCH_W:(BLK_K + 1) * BRANCH_W].reshape(1, db, tnew, HEADS, 2, DA_HEAD_DIM)
    v_s = proj_s[:, BLK_V * BRANCH_W:(BLK_V + 1) * BRANCH_W].reshape(1, db, tnew, HEADS, HEAD_W)

    return (y_p.reshape(nb, seq, D_MODEL), y_s.reshape(db, tnew, D_MODEL),
            k_p, v_p, s_p[None], k_s, v_s, s_s[None])
```

```python
import functools
import math

import jax
import jax.numpy as jnp
from jax import lax
from jax.experimental import pallas as pl
from jax.experimental.pallas import tpu as pltpu

D_MODEL = 2048
HEADS = 8
HEAD_W = 128
DA_HEAD_DIM = 64
BRANCH_W = HEADS * HEAD_W
D_FF = 4 * D_MODEL
PAGE = 128
LN_EPS = 1e-5
RMS_EPS = 1e-6
DEPTH = 1
ALPHA = (2 * DEPTH) ** 0.25
LAM_INIT = 0.8 - 0.6 * math.exp(-0.3 * 0)
IN_WIDTH = 7 * BRANCH_W + 2 * D_MODEL
BLK_Q, BLK_K, BLK_V, BLK_HQ, BLK_GATES = 0, 1, 2, 3, 7
NEG = -1e30

F32 = jnp.float32
BF16 = jnp.bfloat16
VMEM_LIMIT = 52 * 1024 * 1024


def _cparams(sem):
    return pltpu.CompilerParams(dimension_semantics=sem, vmem_limit_bytes=VMEM_LIMIT)


def _sigmoid(x):
    return 1.0 / (1.0 + jnp.exp(-x))


def _proj_kernel(x_ref, w_ref, o_ref, wb):
    @pl.when(pl.program_id(1) == 0)
    def _():
        wb[...] = w_ref[...].astype(BF16)

    o_ref[...] = jnp.dot(x_ref[...], wb[...], preferred_element_type=F32)


def _in_proj(x_bf, w_in, tm, skip_kv):
    t, k = x_bf.shape
    tn = BRANCH_W
    n_blk = IN_WIDTH // tn - (2 if skip_kv else 0)
    wcol = (lambda j: j + 2 * jnp.minimum(j, 1)) if skip_kv else (lambda j: j)
    return pl.pallas_call(
        _proj_kernel,
        grid=(n_blk, t // tm),
        in_specs=[pl.BlockSpec((tm, k), lambda j, i: (i, 0)),
                  pl.BlockSpec((None, k, tn), lambda j, i: (0, 0, wcol(j)))],
        out_specs=pl.BlockSpec((tm, tn), lambda j, i: (i, j)),
        out_shape=jax.ShapeDtypeStruct((t, n_blk * tn), F32),
        scratch_shapes=[pltpu.VMEM((k, tn), BF16)],
        compiler_params=_cparams(("parallel", "arbitrary")),
        name="in_proj",
    )(x_bf, w_in)


def _proj_kv_kernel(x_ref, wk_ref, wv_ref, v_ref, kt_ref, vt_ref, wkb, wvb):
    @pl.when(pl.program_id(0) == 0)
    def _():
        wkb[...] = wk_ref[...].astype(BF16)
        wvb[...] = wv_ref[...].astype(BF16)

    x = x_ref[...]
    tm = x.shape[0]
    kt_ref[...] = jnp.dot(x, wkb[...], preferred_element_type=F32).T
    v = jnp.dot(x, wvb[...], preferred_element_type=F32)
    v_ref[...] = v
    for h in range(HEADS):
        vt_ref[pl.ds(h, tm, stride=HEADS), :] = v[:, h * HEAD_W:(h + 1) * HEAD_W]


def _proj_kv(x_bf, w_in, nb, seq, tm):
    t, k = x_bf.shape
    nt = seq // tm
    wspec = lambda c: pl.BlockSpec((None, k, BRANCH_W), lambda i: (0, 0, c), pipeline_mode=pl.Buffered(1))
    return pl.pallas_call(
        _proj_kv_kernel,
        grid=(t // tm,),
        in_specs=[pl.BlockSpec((tm, k), lambda i: (i, 0)), wspec(1), wspec(2)],
        out_specs=[pl.BlockSpec((tm, BRANCH_W), lambda i: (i, 0)),
                   pl.BlockSpec((None, BRANCH_W, tm), lambda i: (i // nt, 0, i % nt)),
                   pl.BlockSpec((tm * HEADS, HEAD_W), lambda i: (i, 0))],
        out_shape=[jax.ShapeDtypeStruct((t, BRANCH_W), F32),
                   jax.ShapeDtypeStruct((nb, BRANCH_W, seq), F32),
                   jax.ShapeDtypeStruct((t * HEADS, HEAD_W), F32)],
        scratch_shapes=[pltpu.VMEM((k, BRANCH_W), BF16), pltpu.VMEM((k, BRANCH_W), BF16)],
        compiler_params=_cparams(("arbitrary",)),
        name="proj_kv",
    )(x_bf, w_in, w_in)


def _hgrn_kernel(*refs, nbb, tb, c, has_s0):
    if has_s0:
        hq_ref, hf_ref, hi_ref, hg_ref, lb_ref, nw_ref, s0_ref, o_ref, s_ref, st_scr = refs
    else:
        hq_ref, hf_ref, hi_ref, hg_ref, lb_ref, nw_ref, o_ref, s_ref, st_scr = refs
        s0_ref = None
    t = pl.program_id(1)

    @pl.when(t == 0)
    def _():
        for q in range(nbb):
            for h in range(HEADS):
                if has_s0:
                    st_scr[q * HEADS + h] = s0_ref[q, h].T
                else:
                    st_scr[q * HEADS + h] = jnp.zeros((HEAD_W, HEAD_W), F32)

    lbx = lb_ref[...]
    lbe = jnp.exp(lbx - jnp.max(lbx, axis=0, keepdims=True))
    lb_all = lbe[0:1, :] / jnp.sum(lbe, axis=0, keepdims=True)
    nw_t = jnp.concatenate([nw_ref[...]] * HEADS, axis=1)

    row = lax.broadcasted_iota(jnp.int32, (c, c), 0)
    col = lax.broadcasted_iota(jnp.int32, (c, c), 1)
    tri = jnp.where(row >= col, 1.0, 0.0).astype(BF16)
    causal = row >= col

    one_m_lb = 1.0 - lb_all

    def seq_chunk(r, q):
        z = hf_ref[r, :]
        logf = jnp.log(lb_all + one_m_lb * _sigmoid(z))
        kk = one_m_lb * _sigmoid(-z)
        hq = hq_ref[r, :]
        qq = hq * _sigmoid(hq)
        v_bf = hi_ref[r, :].astype(BF16)
        p0 = logf.astype(BF16)
        r1 = logf - p0.astype(F32)
        p1 = r1.astype(BF16)
        p2 = (r1 - p1.astype(F32)).astype(BF16)
        b = (jnp.dot(tri, p0, preferred_element_type=F32)
             + jnp.dot(tri, p1, preferred_element_type=F32)
             + jnp.dot(tri, p2, preferred_element_type=F32))
        b_last = b[c - 1:c, :]
        decay = jnp.exp(b_last)
        qe = (qq * jnp.exp(b)).astype(BF16)
        kd = (kk * jnp.exp(b_last - b)).astype(BF16)
        a = [jnp.zeros((c, c), F32) for _ in range(HEADS)]
        for s in range(c):
            lo = (s // 8) * 8
            d = jnp.minimum(b[lo:, :] - b[s:s + 1, :], 0.0)
            term = qq[lo:, :] * kk[s:s + 1, :] * jnp.exp(d)
            for h in range(HEADS):
                colsum = jnp.sum(term[:, h * HEAD_W:(h + 1) * HEAD_W], axis=-1, keepdims=True)
                if lo:
                    colsum = jnp.concatenate([jnp.zeros((lo, 1), F32), colsum], axis=0)
                a[h] = jnp.where(col == s, colsum, a[h])
        gate = nw_t * _sigmoid(hg_ref[r, :])
        for h in range(HEADS):
            cs = slice(h * HEAD_W, (h + 1) * HEAD_W)
            i = q * HEADS + h
            a_h = jnp.where(causal, a[h], 0.0).astype(BF16)
            o = (jnp.dot(a_h, v_bf[:, cs], preferred_element_type=F32)
                 + lax.dot_general(qe[:, cs], st_scr[i].astype(BF16), (((1,), (1,)), ((), ())),
                                   preferred_element_type=F32))
            u_t = lax.dot_general(v_bf[:, cs], kd[:, cs], (((0,), (0,)), ((), ())),
                                  preferred_element_type=F32)
            st_scr[i] = st_scr[i] * decay[:, cs] + u_t
            rms = lax.rsqrt(jnp.mean(o * o, axis=-1, keepdims=True) + RMS_EPS)
            o_ref[r, cs] = o * rms * gate[:, cs]

    def chunk(ci, carry):
        for q in range(nbb):
            seq_chunk(pl.ds(pl.multiple_of(q * tb + ci * c, c), c), q)
        return carry

    n_chunks = tb // c
    if n_chunks == 1:
        chunk(0, 0)
    else:
        lax.fori_loop(0, n_chunks, chunk, 0, unroll=2)

    @pl.when(t == pl.num_programs(1) - 1)
    def _():
        for q in range(nbb):
            for h in range(HEADS):
                s_ref[q, h] = st_scr[q * HEADS + h].T


def _hgrn(proj, blk_hq, hg_lb, hg_norm_w, s0, nb, seq, nbb, tb, c):
    nt = seq // tb
    assert nbb == 1 or nt == 1
    has_s0 = s0 is not None

    def tok(colblk):
        return pl.BlockSpec((nbb * tb, BRANCH_W), lambda b, t: (b * nt + t, colblk))

    state_spec = pl.BlockSpec((nbb, HEADS, HEAD_W, HEAD_W), lambda b, t: (b, 0, 0, 0))
    in_specs = [tok(blk_hq), tok(blk_hq + 1), tok(blk_hq + 2), tok(blk_hq + 3),
                pl.BlockSpec(hg_lb.shape, lambda b, t: (0, 0)),
                pl.BlockSpec((1, HEAD_W), lambda b, t: (0, 0))]
    args = [proj, proj, proj, proj, hg_lb, hg_norm_w]
    if has_s0:
        in_specs.append(state_spec)
        args.append(s0)
    return pl.pallas_call(
        functools.partial(_hgrn_kernel, nbb=nbb, tb=tb, c=c, has_s0=has_s0),
        grid=(nb // nbb, nt),
        in_specs=in_specs,
        out_specs=[pl.BlockSpec((nbb * tb, BRANCH_W), lambda b, t: (b * nt + t, 0)), state_spec],
        out_shape=[jax.ShapeDtypeStruct((nb * seq, BRANCH_W), F32),
                   jax.ShapeDtypeStruct((nb, HEADS, HEAD_W, HEAD_W), F32)],
        scratch_shapes=[pltpu.VMEM((nbb * HEADS, HEAD_W, HEAD_W), F32)],
        compiler_params=_cparams(("parallel", "arbitrary")),
        name="hgrn_s0" if has_s0 else "hgrn",
    )(*args)


def _lam(q1_ref, k1_ref, q2_ref, k2_ref):
    s1 = jnp.sum(q1_ref[...] * k1_ref[...], axis=-1, keepdims=True)
    s2 = jnp.sum(q2_ref[...] * k2_ref[...], axis=-1, keepdims=True)
    return jnp.exp(s1) - jnp.exp(s2) + LAM_INIT


def _da_finish(o1, o2, lam, nw):
    o = o1 - lam * o2
    rms = lax.rsqrt(jnp.mean(o * o, axis=-1, keepdims=True) + RMS_EPS)
    return o * rms * nw * (1.0 - LAM_INIT)


def _attn_prompt_kernel(q_ref, k_ref, v_ref, lq1, lk1, lq2, lk2, nw_ref, o_ref, kb, vb, *, tq, tk):
    qi = pl.program_id(2)

    @pl.when(qi == 0)
    def _():
        kb[...] = k_ref[...].astype(BF16)
        vb[...] = v_ref[...].astype(BF16)

    q = q_ref[...] * (DA_HEAD_DIM ** -0.5)
    lane = lax.broadcasted_iota(jnp.int32, q.shape, 1)
    qa = jnp.where(lane < DA_HEAD_DIM, q, 0.0).astype(BF16)
    qb = jnp.where(lane >= DA_HEAD_DIM, q, 0.0).astype(BF16)
    row0 = lax.broadcasted_iota(jnp.int32, (tq, tk), 0)
    col0 = lax.broadcasted_iota(jnp.int32, (tq, tk), 1)

    def body(j, carry, masked=False):
        m1, l1, a1, m2, l2, a2 = carry
        ks = pl.ds(pl.multiple_of(j * tk, tk), tk)
        kj = kb[:, ks]
        vj = vb[ks, :]
        outs = []
        for qm, m, l, a in ((qa, m1, l1, a1), (qb, m2, l2, a2)):
            s = jnp.dot(qm, kj, preferred_element_type=F32)
            if masked:
                s = jnp.where(col0 <= row0, s, NEG)
            m_new = jnp.maximum(m, jnp.max(s, axis=-1, keepdims=True))
            corr = jnp.exp(m - m_new)
            p = jnp.exp(s - m_new)
            l_new = l * corr + jnp.sum(p, axis=-1, keepdims=True)
            a_new = a * corr + jnp.dot(p.astype(BF16), vj, preferred_element_type=F32)
            outs += [m_new, l_new, a_new]
        return tuple(outs)

    init = (jnp.full((tq, 1), NEG, F32), jnp.zeros((tq, 1), F32), jnp.zeros((tq, HEAD_W), F32)) * 2
    carry = lax.fori_loop(0, qi, body, init)
    m1, l1, a1, m2, l2, a2 = body(qi, carry, masked=True)
    lam = _lam(lq1, lk1, lq2, lk2)
    o_ref[...] = _da_finish(a1 / l1, a2 / l2, lam, nw_ref[...])


def _attn_prompt(proj, kt, v, lam_q1, lam_k1, lam_q2, lam_k2, da_norm_w, nb, seq, tq, tk):
    assert tq == tk
    nq = seq // tq
    small = lambda shp: pl.BlockSpec(shp, lambda b, h, i: (0, 0))
    return pl.pallas_call(
        functools.partial(_attn_prompt_kernel, tq=tq, tk=tk),
        grid=(nb, HEADS, nq),
        in_specs=[pl.BlockSpec((tq, HEAD_W), lambda b, h, i: (b * nq + i, BLK_Q * HEADS + h)),
                  pl.BlockSpec((None, HEAD_W, seq), lambda b, h, i: (b, h, 0)),
                  pl.BlockSpec((seq, HEAD_W), lambda b, h, i: (b, h)),
                  small((1, DA_HEAD_DIM)), small((1, DA_HEAD_DIM)),
                  small((1, DA_HEAD_DIM)), small((1, DA_HEAD_DIM)),
                  small((1, HEAD_W))],
        out_specs=pl.BlockSpec((tq, HEAD_W), lambda b, h, i: (b * nq + i, h)),
        out_shape=jax.ShapeDtypeStruct((nb * seq, BRANCH_W), F32),
        scratch_shapes=[pltpu.VMEM((HEAD_W, seq), BF16), pltpu.VMEM((seq, HEAD_W), BF16)],
        compiler_params=_cparams(("parallel", "parallel", "arbitrary")),
        name="attn_prompt",
    )(proj, kt, v, lam_q1, lam_k1, lam_q2, lam_k2, da_norm_w)


def _attn_sample_kernel(pt_ref, q_ref, kn_ref, vn_ref, *rest, pp, tnew, mlp_chunks):
    k_refs = rest[:pp]
    v_refs = rest[pp:2 * pp]
    rest = rest[2 * pp:]
    if mlp_chunks:
        (lq1, lk1, lq2, lk2, nw_ref, h_ref, wup_ref, wdn_ref, lw_ref, lb_ref,
         o_ref, y_ref, qblk, kcat, vcat, m_scr, l_scr, acc, hb) = rest
    else:
        lq1, lk1, lq2, lk2, nw_ref, o_ref, qblk, kcat, vcat, m_scr, l_scr, acc = rest
    g = pl.program_id(1)
    rows = HEADS * 2 * tnew
    if mlp_chunks:
        jm = (pl.program_id(0) * pl.num_programs(1) + g) % mlp_chunks
        pl.when(jm == 0)(lambda: _mlp_init(h_ref, y_ref, hb))

    @pl.when(g == 0)
    def _():
        q = q_ref[...] * (DA_HEAD_DIM ** -0.5)
        tiled = jnp.concatenate([q] * (HEADS * 2), axis=0)
        r = lax.broadcasted_iota(jnp.int32, tiled.shape, 0)
        cidx = lax.broadcasted_iota(jnp.int32, tiled.shape, 1)
        qblk[...] = jnp.where((r // tnew) == (cidx // DA_HEAD_DIM), tiled, 0.0).astype(BF16)
        m_scr[...] = jnp.full_like(m_scr, NEG)
        l_scr[...] = jnp.zeros_like(l_scr)
        acc[...] = jnp.zeros_like(acc)

    def update(s, kv_len, between=None):
        m_old = m_scr[...]
        m_new = jnp.maximum(m_old, jnp.max(s, axis=-1, keepdims=True))
        corr = jnp.exp(m_old - m_new)
        p = jnp.exp(s - m_new)
        l_scr[...] = l_scr[...] * corr + jnp.sum(p, axis=-1, keepdims=True)
        m_scr[...] = m_new
        p_bf = p.astype(BF16)
        for h in range(HEADS):
            rs = slice(h * 2 * tnew, (h + 1) * 2 * tnew)
            pv = jnp.dot(p_bf[rs, :], vcat[h, 0:kv_len, :], preferred_element_type=F32)
            acc[rs, :] = acc[rs, :] * corr[rs, :] + pv
            if between is not None:
                between(h)

    kt = D_MODEL // pp
    u = None
    for j in range(pp):
        ls = slice(j * PAGE, (j + 1) * PAGE)
        kcat[:, ls] = k_refs[j][...].astype(BF16)
        for h in range(HEADS):
            vcat[h, ls, :] = v_refs[j][pl.ds(h, PAGE, stride=HEADS), :].astype(BF16)
        if mlp_chunks:
            part = jnp.dot(hb[:, j * kt:(j + 1) * kt], wup_ref[j * kt:(j + 1) * kt, :],
                           preferred_element_type=F32)
            u = part if u is None else u + part
    s = jnp.dot(qblk[...], kcat[...], preferred_element_type=F32)
    if mlp_chunks:
        u = jnp.maximum(u, 0.0)
        u2 = (u * u).astype(BF16)
        nt = D_MODEL // HEADS

        def mlp_piece(h):
            cs = slice(h * nt, (h + 1) * nt)
            y_ref[:, cs] += jnp.dot(u2, wdn_ref[:, cs], preferred_element_type=F32)
    else:
        mlp_piece = None
    update(s, pp * PAGE, mlp_piece)
    if mlp_chunks:
        pl.when(jm == mlp_chunks - 1)(lambda: _mlp_finish(h_ref, lw_ref, lb_ref, y_ref))

    @pl.when(g == pl.num_programs(1) - 1)
    def _():
        pad = jnp.zeros((PAGE - tnew, BRANCH_W), F32)
        kn = jnp.concatenate([kn_ref[...], pad], axis=0).astype(BF16)
        vn = jnp.concatenate([vn_ref[...], pad], axis=0).astype(BF16)
        for h in range(HEADS):
            vcat[h, 0:PAGE, :] = vn[:, h * HEAD_W:(h + 1) * HEAD_W]
        sn = lax.dot_general(qblk[...], kn, (((1,), (1,)), ((), ())),
                             preferred_element_type=F32)
        r = lax.broadcasted_iota(jnp.int32, sn.shape, 0)
        cidx = lax.broadcasted_iota(jnp.int32, sn.shape, 1)
        sn = jnp.where(cidx <= (r % tnew), sn, NEG)
        update(sn, PAGE)
        lam = _lam(lq1, lk1, lq2, lk2)
        o = acc[...] / l_scr[...]
        nw = nw_ref[...]
        for h in range(HEADS):
            o1 = o[h * 2 * tnew:h * 2 * tnew + tnew, :]
            o2 = o[h * 2 * tnew + tnew:(h + 1) * 2 * tnew, :]
            o_ref[:, h * HEAD_W:(h + 1) * HEAD_W] = _da_finish(o1, o2, lam, nw)


def _attn_sample(proj, cache_k, cache_v, page_table, lam_q1, lam_k1, lam_q2, lam_k2, da_norm_w,
                 nb, tnew, pp, mlp=None):
    n_pages = page_table.shape[1]
    ng = n_pages // pp
    ck = cache_k.reshape(-1, PAGE, BRANCH_W).transpose(0, 2, 1)
    cv = cache_v.reshape(-1, PAGE * HEADS, HEAD_W)
    rows = HEADS * 2 * tnew

    def page_spec(j):
        return pl.BlockSpec((None, BRANCH_W, PAGE), lambda b, g, pt: (pt[b, g * pp + j], 0, 0))

    small = lambda shp: pl.BlockSpec(shp, lambda b, g, pt: (0, 0))
    in_specs = ([pl.BlockSpec((tnew, BRANCH_W), lambda b, g, pt: (b, BLK_Q)),
                 pl.BlockSpec((tnew, BRANCH_W), lambda b, g, pt: (b, BLK_K)),
                 pl.BlockSpec((tnew, BRANCH_W), lambda b, g, pt: (b, BLK_V))]
                + [page_spec(j) for j in range(pp)] + [page_spec(j) for j in range(pp)]
                + [small((1, DA_HEAD_DIM))] * 4 + [small((1, HEAD_W))])
    args = [page_table, proj, proj, proj, *([ck] * pp), *([cv] * pp),
            lam_q1, lam_k1, lam_q2, lam_k2, da_norm_w]
    out_specs = [pl.BlockSpec((tnew, BRANCH_W), lambda b, g, pt: (b, 0))]
    out_shape = [jax.ShapeDtypeStruct((nb * tnew, BRANCH_W), F32)]
    scratch = [pltpu.VMEM((rows, BRANCH_W), BF16),
               pltpu.VMEM((BRANCH_W, pp * PAGE), BF16),
               pltpu.VMEM((HEADS, pp * PAGE, HEAD_W), BF16),
               pltpu.VMEM((rows, 1), F32),
               pltpu.VMEM((rows, 1), F32),
               pltpu.VMEM((rows, HEAD_W), F32)]
    mlp_chunks = 0
    if mlp is not None:
        h, wup, wdn, ln_w, ln_b, tf = mlp
        mlp_chunks = D_FF // tf
        t = h.shape[0]
        tm = t * mlp_chunks // (nb * ng)
        assert tm * nb * ng == t * mlp_chunks and tm % 8 == 0
        tile = lambda b, g: (b * ng + g) // mlp_chunks
        chunk = lambda b, g: (b * ng + g) % mlp_chunks
        in_specs += [pl.BlockSpec((tm, D_MODEL), lambda b, g, pt: (tile(b, g), 0),
                                  pipeline_mode=pl.Buffered(1)),
                     pl.BlockSpec((D_MODEL, tf), lambda b, g, pt: (0, chunk(b, g))),
                     pl.BlockSpec((tf, D_MODEL), lambda b, g, pt: (chunk(b, g), 0)),
                     small((1, D_MODEL)), small((1, D_MODEL))]
        args += [h, wup, wdn, ln_w, ln_b]
        out_specs.append(pl.BlockSpec((tm, D_MODEL), lambda b, g, pt: (tile(b, g), 0)))
        out_shape.append(jax.ShapeDtypeStruct((t, D_MODEL), F32))
        scratch.append(pltpu.VMEM((tm, D_MODEL), BF16))
    outs = pl.pallas_call(
        functools.partial(_attn_sample_kernel, pp=pp, tnew=tnew, mlp_chunks=mlp_chunks),
        grid_spec=pltpu.PrefetchScalarGridSpec(
            num_scalar_prefetch=1, grid=(nb, ng), in_specs=in_specs, out_specs=out_specs,
            scratch_shapes=scratch),
        out_shape=out_shape,
        compiler_params=_cparams(("arbitrary", "arbitrary")),
        name="attn_sample_mlp" if mlp is not None else "attn_sample",
    )(*args)
    return outs if mlp is not None else outs[0]


def _layernorm(x, w, b):
    mu = jnp.mean(x, axis=-1, keepdims=True)
    xc = x - mu
    var = jnp.mean(xc * xc, axis=-1, keepdims=True)
    return xc * lax.rsqrt(var + LN_EPS) * w + b


def _merge_kernel(x_ref, ohg_ref, oda_ref, ga0_ref, ga1_ref, gb0_ref, gb1_ref, wpa_ref, wpb_ref,
                  wout_ref, lw_ref, lb_ref, h_ref):
    pa = jnp.dot(ohg_ref[...].astype(BF16), wpa_ref[...], preferred_element_type=F32)
    pb = jnp.dot(oda_ref[...].astype(BF16), wpb_ref[...], preferred_element_type=F32)
    half = BRANCH_W
    merged = jnp.concatenate(
        [_sigmoid(ga0_ref[...]) * pa[:, :half] + _sigmoid(gb0_ref[...]) * pb[:, :half],
         _sigmoid(ga1_ref[...]) * pa[:, half:] + _sigmoid(gb1_ref[...]) * pb[:, half:]], axis=1)
    y = ALPHA * x_ref[...] + jnp.dot(merged.astype(BF16), wout_ref[...], preferred_element_type=F32)
    h_ref[...] = _layernorm(y, lw_ref[...], lb_ref[...])


def _merge(x, o_hg, o_da, proj, blk_gates, wpa, wpb, wout, ln_w, ln_b, tm):
    t = x.shape[0]
    rowblk = lambda w: pl.BlockSpec((tm, w), lambda i: (i, 0))
    gate = lambda c: pl.BlockSpec((tm, BRANCH_W), lambda i: (i, blk_gates + c))
    const = lambda shp: pl.BlockSpec(shp, lambda i: (0, 0), pipeline_mode=pl.Buffered(1))
    return pl.pallas_call(
        _merge_kernel,
        grid=(t // tm,),
        in_specs=[rowblk(D_MODEL), rowblk(BRANCH_W), rowblk(BRANCH_W),
                  gate(0), gate(1), gate(2), gate(3),
                  const((BRANCH_W, D_MODEL)), const((BRANCH_W, D_MODEL)), const((D_MODEL, D_MODEL)),
                  const((1, D_MODEL)), const((1, D_MODEL))],
        out_specs=rowblk(D_MODEL),
        out_shape=jax.ShapeDtypeStruct((t, D_MODEL), F32),
        compiler_params=_cparams(("parallel",)),
        name="merge_out_ln",
    )(x, o_hg, o_da, proj, proj, proj, proj, wpa, wpb, wout, ln_w, ln_b)


def _mlp_init(h_ref, y_ref, hb):
    hb[...] = h_ref[...].astype(BF16)
    y_ref[...] = jnp.zeros_like(y_ref)


def _mlp_chunk(wup_ref, wdn_ref, y_ref, hb):
    u = jnp.dot(hb[...], wup_ref[...], preferred_element_type=F32)
    u = jnp.maximum(u, 0.0)
    y_ref[...] += jnp.dot((u * u).astype(BF16), wdn_ref[...], preferred_element_type=F32)


def _mlp_finish(h_ref, lw_ref, lb_ref, y_ref):
    y_ref[...] = _layernorm(ALPHA * h_ref[...] + y_ref[...], lw_ref[...], lb_ref[...])


def _mlp_kernel(h_ref, wup_ref, wdn_ref, lw_ref, lb_ref, y_ref, hb):
    j = pl.program_id(1)
    pl.when(j == 0)(lambda: _mlp_init(h_ref, y_ref, hb))
    _mlp_chunk(wup_ref, wdn_ref, y_ref, hb)
    pl.when(j == pl.num_programs(1) - 1)(lambda: _mlp_finish(h_ref, lw_ref, lb_ref, y_ref))


def _mlp(h, wup, wdn, ln_w, ln_b, tm, tf):
    t = h.shape[0]
    return pl.pallas_call(
        _mlp_kernel,
        grid=(t // tm, D_FF // tf),
        in_specs=[pl.BlockSpec((tm, D_MODEL), lambda i, j: (i, 0), pipeline_mode=pl.Buffered(1)),
                  pl.BlockSpec((D_MODEL, tf), lambda i, j: (0, j)),
                  pl.BlockSpec((tf, D_MODEL), lambda i, j: (j, 0)),
                  pl.BlockSpec((1, D_MODEL), lambda i, j: (0, 0)),
                  pl.BlockSpec((1, D_MODEL), lambda i, j: (0, 0))],
        out_specs=pl.BlockSpec((tm, D_MODEL), lambda i, j: (i, 0)),
        out_shape=jax.ShapeDtypeStruct((t, D_MODEL), F32),
        scratch_shapes=[pltpu.VMEM((tm, D_MODEL), BF16)],
        compiler_params=_cparams(("parallel", "arbitrary")),
        name="mlp_ln",
    )(h, wup, wdn, ln_w, ln_b)


def kernel(x_prompt, x_sample, cache_k, cache_v, state_hgrn, page_table, w_in, hg_lb, hg_norm_w,
           lam_q1, lam_k1, lam_q2, lam_k2, da_norm_w, w_pa, w_pb, w_out, ln1_w, ln1_b,
           w_up, w_down, ln2_w, ln2_b):
    nb, seq, _ = x_prompt.shape
    db, tnew, _ = x_sample.shape
    wpa = w_pa[0].astype(BF16)
    wpb = w_pb[0].astype(BF16)
    wout = w_out[0].astype(BF16)
    wup = w_up[0].astype(BF16)
    wdn = w_down[0].astype(BF16)
    lam_args = (lam_q1, lam_k1, lam_q2, lam_k2, da_norm_w)

    xp = x_prompt.reshape(nb * seq, D_MODEL)
    xp_bf = xp.astype(BF16)
    v_p2d, kt_p, vt_p = _proj_kv(xp_bf, w_in, nb, seq, 512)
    proj_p = _in_proj(xp_bf, w_in, 1024, skip_kv=True)
    o_hg_p, s_p = _hgrn(proj_p, BLK_HQ - 2, hg_lb, hg_norm_w, None, nb, seq, 1, 256, 16)
    o_da_p = _attn_prompt(proj_p, kt_p, v_p2d, *lam_args, nb, seq, 512, 512)
    h_p = _merge(xp, o_hg_p, o_da_p, proj_p, BLK_GATES - 2, wpa, wpb, wout, ln1_w, ln1_b, 256)
    k_p = kt_p.reshape(nb, HEADS, 2, DA_HEAD_DIM, seq).transpose(0, 4, 1, 2, 3)[None]
    v_p = vt_p.reshape(1, nb, seq, HEADS, HEAD_W)

    xs = x_sample.reshape(db * tnew, D_MODEL)
    proj_s = _in_proj(xs.astype(BF16), w_in, db * tnew, skip_kv=False)
    o_hg_s, s_s = _hgrn(proj_s, BLK_HQ, hg_lb, hg_norm_w,
                        state_hgrn.reshape(db, HEADS, HEAD_W, HEAD_W), db, tnew, 4, tnew, tnew)
    o_da_s, y_p = _attn_sample(proj_s, cache_k, cache_v, page_table, *lam_args, db, tnew, 8,
                               mlp=(h_p, wup, wdn, ln2_w, ln2_b, 256))
    h_s = _merge(xs, o_hg_s, o_da_s, proj_s, BLK_GATES, wpa, wpb, wout, ln1_w, ln1_b, 256)
    y_s = _mlp(h_s, wup, wdn, ln2_w, ln2_b, 256, 512)
    k_s = proj_s[:, BLK_K * BRANCH_W:(BLK_K + 1) * BRANCH_W].reshape(1, db, tnew, HEADS, 2, DA_HEAD_DIM)
    v_s = proj_s[:, BLK_V * BRANCH_W:(BLK_V + 1) * BRANCH_W].reshape(1, db, tnew, HEADS, HEAD_W)

    return (y_p.reshape(nb, seq, D_MODEL), y_s.reshape(db, tnew, D_MODEL),
            k_p, v_p, s_p[None], k_s, v_s, s_s[None])
```

```python
import functools
import math

import jax
import jax.numpy as jnp
from jax import lax
from jax.experimental import pallas as pl
from jax.experimental.pallas import tpu as pltpu

D_MODEL = 2048
HEADS = 8
HEAD_W = 128
DA_HEAD_DIM = 64
BRANCH_W = HEADS * HEAD_W
D_FF = 4 * D_MODEL
PAGE = 128
LN_EPS = 1e-5
RMS_EPS = 1e-6
DEPTH = 1
ALPHA = (2 * DEPTH) ** 0.25
LAM_INIT = 0.8 - 0.6 * math.exp(-0.3 * 0)
IN_WIDTH = 7 * BRANCH_W + 2 * D_MODEL
BLK_Q, BLK_K, BLK_V, BLK_HQ, BLK_GATES = 0, 1, 2, 3, 7
HGRN_GROUP = 8
NEG = -1e30

F32 = jnp.float32
BF16 = jnp.bfloat16
VMEM_LIMIT = 52 * 1024 * 1024


def _cparams(sem):
    return pltpu.CompilerParams(dimension_semantics=sem, vmem_limit_bytes=VMEM_LIMIT)


def _sigmoid(x):
    return 1.0 / (1.0 + jnp.exp(-x))


def _proj_kernel(x_ref, w_ref, o_ref, wb):
    @pl.when(pl.program_id(1) == 0)
    def _():
        wb[...] = w_ref[...].astype(BF16)

    o_ref[...] = jnp.dot(x_ref[...], wb[...], preferred_element_type=F32)


def _in_proj(x_bf, w_in, tm, skip_kv):
    t, k = x_bf.shape
    tn = BRANCH_W
    n_blk = IN_WIDTH // tn - (2 if skip_kv else 0)
    wcol = (lambda j: j + 2 * jnp.minimum(j, 1)) if skip_kv else (lambda j: j)
    return pl.pallas_call(
        _proj_kernel,
        grid=(n_blk, t // tm),
        in_specs=[pl.BlockSpec((tm, k), lambda j, i: (i, 0)),
                  pl.BlockSpec((None, k, tn), lambda j, i: (0, 0, wcol(j)))],
        out_specs=pl.BlockSpec((tm, tn), lambda j, i: (i, j)),
        out_shape=jax.ShapeDtypeStruct((t, n_blk * tn), F32),
        scratch_shapes=[pltpu.VMEM((k, tn), BF16)],
        compiler_params=_cparams(("parallel", "arbitrary")),
        name="in_proj",
    )(x_bf, w_in)


def _proj_kv_kernel(x_ref, wk_ref, wv_ref, v_ref, kt_ref, vt_ref, wkb, wvb):
    @pl.when(pl.program_id(0) == 0)
    def _():
        wkb[...] = wk_ref[...].astype(BF16)
        wvb[...] = wv_ref[...].astype(BF16)

    x = x_ref[...]
    tm = x.shape[0]
    kt_ref[...] = jnp.dot(x, wkb[...], preferred_element_type=F32).T
    v = jnp.dot(x, wvb[...], preferred_element_type=F32)
    v_ref[...] = v
    for h in range(HEADS):
        vt_ref[pl.ds(h, tm, stride=HEADS), :] = v[:, h * HEAD_W:(h + 1) * HEAD_W]


def _proj_kv(x_bf, w_in, nb, seq, tm):
    t, k = x_bf.shape
    nt = seq // tm
    wspec = lambda c: pl.BlockSpec((None, k, BRANCH_W), lambda i: (0, 0, c), pipeline_mode=pl.Buffered(1))
    return pl.pallas_call(
        _proj_kv_kernel,
        grid=(t // tm,),
        in_specs=[pl.BlockSpec((tm, k), lambda i: (i, 0)), wspec(1), wspec(2)],
        out_specs=[pl.BlockSpec((tm, BRANCH_W), lambda i: (i, 0)),
                   pl.BlockSpec((None, BRANCH_W, tm), lambda i: (i // nt, 0, i % nt)),
                   pl.BlockSpec((tm * HEADS, HEAD_W), lambda i: (i, 0))],
        out_shape=[jax.ShapeDtypeStruct((t, BRANCH_W), F32),
                   jax.ShapeDtypeStruct((nb, BRANCH_W, seq), F32),
                   jax.ShapeDtypeStruct((t * HEADS, HEAD_W), F32)],
        scratch_shapes=[pltpu.VMEM((k, BRANCH_W), BF16), pltpu.VMEM((k, BRANCH_W), BF16)],
        compiler_params=_cparams(("arbitrary",)),
        name="proj_kv",
    )(x_bf, w_in, w_in)


def _hgrn_kernel(*refs, nbb, tb, c, has_s0):
    if has_s0:
        hq_ref, hf_ref, hi_ref, hg_ref, lb_ref, nw_ref, s0_ref, o_ref, s_ref, st_scr = refs
    else:
        hq_ref, hf_ref, hi_ref, hg_ref, lb_ref, nw_ref, o_ref, s_ref, st_scr = refs
        s0_ref = None
    t = pl.program_id(1)

    @pl.when(t == 0)
    def _():
        for q in range(nbb):
            for h in range(HEADS):
                if has_s0:
                    st_scr[q * HEADS + h] = s0_ref[q, h].T
                else:
                    st_scr[q * HEADS + h] = jnp.zeros((HEAD_W, HEAD_W), F32)

    lbx = lb_ref[...]
    lbe = jnp.exp(lbx - jnp.max(lbx, axis=0, keepdims=True))
    lb_all = lbe[0:1, :] / jnp.sum(lbe, axis=0, keepdims=True)
    nw_t = jnp.concatenate([nw_ref[...]] * HEADS, axis=1)

    row = lax.broadcasted_iota(jnp.int32, (c, c), 0)
    col = lax.broadcasted_iota(jnp.int32, (c, c), 1)
    tri = jnp.where(row >= col, 1.0, 0.0).astype(BF16)
    causal = row >= col

    one_m_lb = 1.0 - lb_all

    def chunk_group(items):
        pre = []
        for r, _ in items:
            z = hf_ref[r, :]
            logf = jnp.log(lb_all + one_m_lb * _sigmoid(z))
            kk = one_m_lb * _sigmoid(-z)
            hq = hq_ref[r, :]
            qq = hq * _sigmoid(hq)
            v_bf = hi_ref[r, :].astype(BF16)
            p0 = logf.astype(BF16)
            r1 = logf - p0.astype(F32)
            p1 = r1.astype(BF16)
            p2 = (r1 - p1.astype(F32)).astype(BF16)
            b = (jnp.dot(tri, p0, preferred_element_type=F32)
                 + jnp.dot(tri, p1, preferred_element_type=F32)
                 + jnp.dot(tri, p2, preferred_element_type=F32))
            pre.append((b, qq, kk, v_bf))
        a = [[jnp.zeros((c, c), F32) for _ in range(HEADS)] for _ in items]
        for s in range(c):
            lo = (s // 8) * 8
            for n, (b, qq, kk, _) in enumerate(pre):
                d = jnp.minimum(b[lo:, :] - b[s:s + 1, :], 0.0)
                term = qq[lo:, :] * kk[s:s + 1, :] * jnp.exp(d)
                for h in range(HEADS):
                    colsum = jnp.sum(term[:, h * HEAD_W:(h + 1) * HEAD_W], axis=-1, keepdims=True)
                    if lo:
                        colsum = jnp.concatenate([jnp.zeros((lo, 1), F32), colsum], axis=0)
                    a[n][h] = jnp.where(col == s, colsum, a[n][h])
        for n, ((r, q), (b, qq, kk, v_bf)) in enumerate(zip(items, pre)):
            b_last = b[c - 1:c, :]
            decay = jnp.exp(b_last)
            qe = (qq * jnp.exp(b)).astype(BF16)
            kd = (kk * jnp.exp(b_last - b)).astype(BF16)
            gate = nw_t * _sigmoid(hg_ref[r, :])
            for h in range(HEADS):
                cs = slice(h * HEAD_W, (h + 1) * HEAD_W)
                i = q * HEADS + h
                a_h = jnp.where(causal, a[n][h], 0.0).astype(BF16)
                o = (jnp.dot(a_h, v_bf[:, cs], preferred_element_type=F32)
                     + lax.dot_general(qe[:, cs], st_scr[i].astype(BF16), (((1,), (1,)), ((), ())),
                                       preferred_element_type=F32))
                u_t = lax.dot_general(v_bf[:, cs], kd[:, cs], (((0,), (0,)), ((), ())),
                                      preferred_element_type=F32)
                st_scr[i] = st_scr[i] * decay[:, cs] + u_t
                rms = lax.rsqrt(jnp.mean(o * o, axis=-1, keepdims=True) + RMS_EPS)
                o_ref[r, cs] = o * rms * gate[:, cs]

    n_chunks = tb // c
    if n_chunks == 1:
        chunk_group([(pl.ds(q * tb, c), q) for q in range(nbb)])
    else:
        group = math.gcd(n_chunks, HGRN_GROUP)
        assert nbb == 1

        def trip(gi, carry):
            base = gi * (group * c)
            chunk_group([(pl.ds(pl.multiple_of(base + n * c, c), c), 0) for n in range(group)])
            return carry

        lax.fori_loop(0, n_chunks // group, trip, 0)

    @pl.when(t == pl.num_programs(1) - 1)
    def _():
        for q in range(nbb):
            for h in range(HEADS):
                s_ref[q, h] = st_scr[q * HEADS + h].T


def _hgrn(proj, blk_hq, hg_lb, hg_norm_w, s0, nb, seq, nbb, tb, c):
    nt = seq // tb
    assert nbb == 1 or nt == 1
    has_s0 = s0 is not None

    def tok(colblk):
        return pl.BlockSpec((nbb * tb, BRANCH_W), lambda b, t: (b * nt + t, colblk))

    state_spec = pl.BlockSpec((nbb, HEADS, HEAD_W, HEAD_W), lambda b, t: (b, 0, 0, 0))
    in_specs = [tok(blk_hq), tok(blk_hq + 1), tok(blk_hq + 2), tok(blk_hq + 3),
                pl.BlockSpec(hg_lb.shape, lambda b, t: (0, 0)),
                pl.BlockSpec((1, HEAD_W), lambda b, t: (0, 0))]
    args = [proj, proj, proj, proj, hg_lb, hg_norm_w]
    if has_s0:
        in_specs.append(state_spec)
        args.append(s0)
    return pl.pallas_call(
        functools.partial(_hgrn_kernel, nbb=nbb, tb=tb, c=c, has_s0=has_s0),
        grid=(nb // nbb, nt),
        in_specs=in_specs,
        out_specs=[pl.BlockSpec((nbb * tb, BRANCH_W), lambda b, t: (b * nt + t, 0)), state_spec],
        out_shape=[jax.ShapeDtypeStruct((nb * seq, BRANCH_W), F32),
                   jax.ShapeDtypeStruct((nb, HEADS, HEAD_W, HEAD_W), F32)],
        scratch_shapes=[pltpu.VMEM((nbb * HEADS, HEAD_W, HEAD_W), F32)],
        compiler_params=_cparams(("parallel", "arbitrary")),
        name="hgrn_s0" if has_s0 else "hgrn",
    )(*args)


def _lam(q1_ref, k1_ref, q2_ref, k2_ref):
    s1 = jnp.sum(q1_ref[...] * k1_ref[...], axis=-1, keepdims=True)
    s2 = jnp.sum(q2_ref[...] * k2_ref[...], axis=-1, keepdims=True)
    return jnp.exp(s1) - jnp.exp(s2) + LAM_INIT


def _da_finish(o1, o2, lam, nw):
    o = o1 - lam * o2
    rms = lax.rsqrt(jnp.mean(o * o, axis=-1, keepdims=True) + RMS_EPS)
    return o * rms * nw * (1.0 - LAM_INIT)


def _attn_prompt_kernel(q_ref, k_ref, v_ref, lq1, lk1, lq2, lk2, nw_ref, o_ref, kb, vb, *, tq, tk):
    qi = pl.program_id(2)

    @pl.when(qi == 0)
    def _():
        kb[...] = k_ref[...].astype(BF16)
        vb[...] = v_ref[...].astype(BF16)

    q = q_ref[...] * (DA_HEAD_DIM ** -0.5)
    lane = lax.broadcasted_iota(jnp.int32, q.shape, 1)
    qa = jnp.where(lane < DA_HEAD_DIM, q, 0.0).astype(BF16)
    qb = jnp.where(lane >= DA_HEAD_DIM, q, 0.0).astype(BF16)
    row0 = lax.broadcasted_iota(jnp.int32, (tq, tk), 0)
    col0 = lax.broadcasted_iota(jnp.int32, (tq, tk), 1)

    def body(j, carry, masked=False):
        m1, l1, a1, m2, l2, a2 = carry
        ks = pl.ds(pl.multiple_of(j * tk, tk), tk)
        kj = kb[:, ks]
        vj = vb[ks, :]
        outs = []
        for qm, m, l, a in ((qa, m1, l1, a1), (qb, m2, l2, a2)):
            s = jnp.dot(qm, kj, preferred_element_type=F32)
            if masked:
                s = jnp.where(col0 <= row0, s, NEG)
            m_new = jnp.maximum(m, jnp.max(s, axis=-1, keepdims=True))
            corr = jnp.exp(m - m_new)
            p = jnp.exp(s - m_new)
            l_new = l * corr + jnp.sum(p, axis=-1, keepdims=True)
            a_new = a * corr + jnp.dot(p.astype(BF16), vj, preferred_element_type=F32)
            outs += [m_new, l_new, a_new]
        return tuple(outs)

    init = (jnp.full((tq, 1), NEG, F32), jnp.zeros((tq, 1), F32), jnp.zeros((tq, HEAD_W), F32)) * 2
    carry = lax.fori_loop(0, qi, body, init)
    m1, l1, a1, m2, l2, a2 = body(qi, carry, masked=True)
    lam = _lam(lq1, lk1, lq2, lk2)
    o_ref[...] = _da_finish(a1 / l1, a2 / l2, lam, nw_ref[...])


def _attn_prompt(proj, kt, v, lam_q1, lam_k1, lam_q2, lam_k2, da_norm_w, nb, seq, tq, tk):
    assert tq == tk
    nq = seq // tq
    small = lambda shp: pl.BlockSpec(shp, lambda b, h, i: (0, 0))
    return pl.pallas_call(
        functools.partial(_attn_prompt_kernel, tq=tq, tk=tk),
        grid=(nb, HEADS, nq),
        in_specs=[pl.BlockSpec((tq, HEAD_W), lambda b, h, i: (b * nq + i, BLK_Q * HEADS + h)),
                  pl.BlockSpec((None, HEAD_W, seq), lambda b, h, i: (b, h, 0)),
                  pl.BlockSpec((seq, HEAD_W), lambda b, h, i: (b, h)),
                  small((1, DA_HEAD_DIM)), small((1, DA_HEAD_DIM)),
                  small((1, DA_HEAD_DIM)), small((1, DA_HEAD_DIM)),
                  small((1, HEAD_W))],
        out_specs=pl.BlockSpec((tq, HEAD_W), lambda b, h, i: (b * nq + i, h)),
        out_shape=jax.ShapeDtypeStruct((nb * seq, BRANCH_W), F32),
        scratch_shapes=[pltpu.VMEM((HEAD_W, seq), BF16), pltpu.VMEM((seq, HEAD_W), BF16)],
        compiler_params=_cparams(("parallel", "parallel", "arbitrary")),
        name="attn_prompt",
    )(proj, kt, v, lam_q1, lam_k1, lam_q2, lam_k2, da_norm_w)


def _attn_sample_kernel(pt_ref, q_ref, kn_ref, vn_ref, *rest, pp, tnew):
    k_refs = rest[:pp]
    v_refs = rest[pp:2 * pp]
    lq1, lk1, lq2, lk2, nw_ref, o_ref, qblk, kcat, vcat, m_scr, l_scr, acc = rest[2 * pp:]
    g = pl.program_id(1)
    rows = HEADS * 2 * tnew

    @pl.when(g == 0)
    def _():
        q = q_ref[...] * (DA_HEAD_DIM ** -0.5)
        tiled = jnp.concatenate([q] * (HEADS * 2), axis=0)
        r = lax.broadcasted_iota(jnp.int32, tiled.shape, 0)
        cidx = lax.broadcasted_iota(jnp.int32, tiled.shape, 1)
        qblk[...] = jnp.where((r // tnew) == (cidx // DA_HEAD_DIM), tiled, 0.0).astype(BF16)
        m_scr[...] = jnp.full_like(m_scr, NEG)
        l_scr[...] = jnp.zeros_like(l_scr)
        acc[...] = jnp.zeros_like(acc)

    def update(s, kv_len):
        m_old = m_scr[...]
        m_new = jnp.maximum(m_old, jnp.max(s, axis=-1, keepdims=True))
        corr = jnp.exp(m_old - m_new)
        p = jnp.exp(s - m_new)
        l_scr[...] = l_scr[...] * corr + jnp.sum(p, axis=-1, keepdims=True)
        m_scr[...] = m_new
        p_bf = p.astype(BF16)
        for h in range(HEADS):
            rs = slice(h * 2 * tnew, (h + 1) * 2 * tnew)
            pv = jnp.dot(p_bf[rs, :], vcat[h, 0:kv_len, :], preferred_element_type=F32)
            acc[rs, :] = acc[rs, :] * corr[rs, :] + pv

    for j in range(pp):
        ls = slice(j * PAGE, (j + 1) * PAGE)
        kcat[:, ls] = k_refs[j][...].astype(BF16)
        for h in range(HEADS):
            vcat[h, ls, :] = v_refs[j][pl.ds(h, PAGE, stride=HEADS), :].astype(BF16)
    s = jnp.dot(qblk[...], kcat[...], preferred_element_type=F32)
    update(s, pp * PAGE)

    @pl.when(g == pl.num_programs(1) - 1)
    def _():
        pad = jnp.zeros((PAGE - tnew, BRANCH_W), F32)
        kn = jnp.concatenate([kn_ref[...], pad], axis=0).astype(BF16)
        vn = jnp.concatenate([vn_ref[...], pad], axis=0).astype(BF16)
        for h in range(HEADS):
            vcat[h, 0:PAGE, :] = vn[:, h * HEAD_W:(h + 1) * HEAD_W]
        sn = lax.dot_general(qblk[...], kn, (((1,), (1,)), ((), ())),
                             preferred_element_type=F32)
        r = lax.broadcasted_iota(jnp.int32, sn.shape, 0)
        cidx = lax.broadcasted_iota(jnp.int32, sn.shape, 1)
        sn = jnp.where(cidx <= (r % tnew), sn, NEG)
        update(sn, PAGE)
        lam = _lam(lq1, lk1, lq2, lk2)
        o = acc[...] / l_scr[...]
        nw = nw_ref[...]
        for h in range(HEADS):
            o1 = o[h * 2 * tnew:h * 2 * tnew + tnew, :]
            o2 = o[h * 2 * tnew + tnew:(h + 1) * 2 * tnew, :]
            o_ref[:, h * HEAD_W:(h + 1) * HEAD_W] = _da_finish(o1, o2, lam, nw)


def _attn_sample(proj, cache_k, cache_v, page_table, lam_q1, lam_k1, lam_q2, lam_k2, da_norm_w,
                 nb, tnew, pp):
    n_pages = page_table.shape[1]
    ng = n_pages // pp
    ck = cache_k.reshape(-1, PAGE, BRANCH_W).transpose(0, 2, 1)
    cv = cache_v.reshape(-1, PAGE * HEADS, HEAD_W)
    rows = HEADS * 2 * tnew

    def page_spec(j):
        return pl.BlockSpec((None, BRANCH_W, PAGE), lambda b, g, pt: (pt[b, g * pp + j], 0, 0))

    small = lambda shp: pl.BlockSpec(shp, lambda b, g, pt: (0, 0))
    in_specs = ([pl.BlockSpec((tnew, BRANCH_W), lambda b, g, pt: (b, BLK_Q)),
                 pl.BlockSpec((tnew, BRANCH_W), lambda b, g, pt: (b, BLK_K)),
                 pl.BlockSpec((tnew, BRANCH_W), lambda b, g, pt: (b, BLK_V))]
                + [page_spec(j) for j in range(pp)] + [page_spec(j) for j in range(pp)]
                + [small((1, DA_HEAD_DIM))] * 4 + [small((1, HEAD_W))])
    args = [page_table, proj, proj, proj, *([ck] * pp), *([cv] * pp),
            lam_q1, lam_k1, lam_q2, lam_k2, da_norm_w]
    scratch = [pltpu.VMEM((rows, BRANCH_W), BF16),
               pltpu.VMEM((BRANCH_W, pp * PAGE), BF16),
               pltpu.VMEM((HEADS, pp * PAGE, HEAD_W), BF16),
               pltpu.VMEM((rows, 1), F32),
               pltpu.VMEM((rows, 1), F32),
               pltpu.VMEM((rows, HEAD_W), F32)]
    return pl.pallas_call(
        functools.partial(_attn_sample_kernel, pp=pp, tnew=tnew),
        grid_spec=pltpu.PrefetchScalarGridSpec(
            num_scalar_prefetch=1, grid=(nb, ng), in_specs=in_specs,
            out_specs=pl.BlockSpec((tnew, BRANCH_W), lambda b, g, pt: (b, 0)),
            scratch_shapes=scratch),
        out_shape=jax.ShapeDtypeStruct((nb * tnew, BRANCH_W), F32),
        compiler_params=_cparams(("parallel", "arbitrary")),
        name="attn_sample",
    )(*args)


def _layernorm(x, w, b):
    mu = jnp.mean(x, axis=-1, keepdims=True)
    xc = x - mu
    var = jnp.mean(xc * xc, axis=-1, keepdims=True)
    return xc * lax.rsqrt(var + LN_EPS) * w + b


def _merge_kernel(x_ref, ohg_ref, oda_ref, ga0_ref, ga1_ref, gb0_ref, gb1_ref, wpa_ref, wpb_ref,
                  wout_ref, lw_ref, lb_ref, h_ref):
    pa = jnp.dot(ohg_ref[...].astype(BF16), wpa_ref[...], preferred_element_type=F32)
    pb = jnp.dot(oda_ref[...].astype(BF16), wpb_ref[...], preferred_element_type=F32)
    half = BRANCH_W
    merged = jnp.concatenate(
        [_sigmoid(ga0_ref[...]) * pa[:, :half] + _sigmoid(gb0_ref[...]) * pb[:, :half],
         _sigmoid(ga1_ref[...]) * pa[:, half:] + _sigmoid(gb1_ref[...]) * pb[:, half:]], axis=1)
    y = ALPHA * x_ref[...] + jnp.dot(merged.astype(BF16), wout_ref[...], preferred_element_type=F32)
    h_ref[...] = _layernorm(y, lw_ref[...], lb_ref[...])


def _merge(x, o_hg, o_da, proj, blk_gates, wpa, wpb, wout, ln_w, ln_b, tm):
    t = x.shape[0]
    rowblk = lambda w: pl.BlockSpec((tm, w), lambda i: (i, 0))
    gate = lambda c: pl.BlockSpec((tm, BRANCH_W), lambda i: (i, blk_gates + c))
    const = lambda shp: pl.BlockSpec(shp, lambda i: (0, 0), pipeline_mode=pl.Buffered(1))
    return pl.pallas_call(
        _merge_kernel,
        grid=(t // tm,),
        in_specs=[rowblk(D_MODEL), rowblk(BRANCH_W), rowblk(BRANCH_W),
                  gate(0), gate(1), gate(2), gate(3),
                  const((BRANCH_W, D_MODEL)), const((BRANCH_W, D_MODEL)), const((D_MODEL, D_MODEL)),
                  const((1, D_MODEL)), const((1, D_MODEL))],
        out_specs=rowblk(D_MODEL),
        out_shape=jax.ShapeDtypeStruct((t, D_MODEL), F32),
        compiler_params=_cparams(("parallel",)),
        name="merge_out_ln",
    )(x, o_hg, o_da, proj, proj, proj, proj, wpa, wpb, wout, ln_w, ln_b)


def _mlp_init(h_ref, y_ref, hb):
    hb[...] = h_ref[...].astype(BF16)
    y_ref[...] = jnp.zeros_like(y_ref)


def _mlp_chunk(wup_ref, wdn_ref, y_ref, hb):
    u = jnp.dot(hb[...], wup_ref[...], preferred_element_type=F32)
    u = jnp.maximum(u, 0.0)
    y_ref[...] += jnp.dot((u * u).astype(BF16), wdn_ref[...], preferred_element_type=F32)


def _mlp_finish(h_ref, lw_ref, lb_ref, y_ref):
    y_ref[...] = _layernorm(ALPHA * h_ref[...] + y_ref[...], lw_ref[...], lb_ref[...])


def _mlp_kernel(h_ref, wup_ref, wdn_ref, lw_ref, lb_ref, y_ref, hb):
    j = pl.program_id(1)
    pl.when(j == 0)(lambda: _mlp_init(h_ref, y_ref, hb))
    _mlp_chunk(wup_ref, wdn_ref, y_ref, hb)
    pl.when(j == pl.num_programs(1) - 1)(lambda: _mlp_finish(h_ref, lw_ref, lb_ref, y_ref))


def _mlp(h, wup, wdn, ln_w, ln_b, tm, tf):
    t = h.shape[0]
    return pl.pallas_call(
        _mlp_kernel,
        grid=(t // tm, D_FF // tf),
        in_specs=[pl.BlockSpec((tm, D_MODEL), lambda i, j: (i, 0), pipeline_mode=pl.Buffered(1)),
                  pl.BlockSpec((D_MODEL, tf), lambda i, j: (0, j)),
                  pl.BlockSpec((tf, D_MODEL), lambda i, j: (j, 0)),
                  pl.BlockSpec((1, D_MODEL), lambda i, j: (0, 0)),
                  pl.BlockSpec((1, D_MODEL), lambda i, j: (0, 0))],
        out_specs=pl.BlockSpec((tm, D_MODEL), lambda i, j: (i, 0)),
        out_shape=jax.ShapeDtypeStruct((t, D_MODEL), F32),
        scratch_shapes=[pltpu.VMEM((tm, D_MODEL), BF16)],
        compiler_params=_cparams(("parallel", "arbitrary")),
        name="mlp_ln",
    )(h, wup, wdn, ln_w, ln_b)


def kernel(x_prompt, x_sample, cache_k, cache_v, state_hgrn, page_table, w_in, hg_lb, hg_norm_w,
           lam_q1, lam_k1, lam_q2, lam_k2, da_norm_w, w_pa, w_pb, w_out, ln1_w, ln1_b,
           w_up, w_down, ln2_w, ln2_b):
    nb, seq, _ = x_prompt.shape
    db, tnew, _ = x_sample.shape
    wpa = w_pa[0].astype(BF16)
    wpb = w_pb[0].astype(BF16)
    wout = w_out[0].astype(BF16)
    wup = w_up[0].astype(BF16)
    wdn = w_down[0].astype(BF16)
    lam_args = (lam_q1, lam_k1, lam_q2, lam_k2, da_norm_w)

    xp = x_prompt.reshape(nb * seq, D_MODEL)
    xp_bf = xp.astype(BF16)
    v_p2d, kt_p, vt_p = _proj_kv(xp_bf, w_in, nb, seq, 512)
    proj_p = _in_proj(xp_bf, w_in, 1024, skip_kv=True)
    o_hg_p, s_p = _hgrn(proj_p, BLK_HQ - 2, hg_lb, hg_norm_w, None, nb, seq, 1, 256, 16)
    o_da_p = _attn_prompt(proj_p, kt_p, v_p2d, *lam_args, nb, seq, 1024, 1024)
    h_p = _merge(xp, o_hg_p, o_da_p, proj_p, BLK_GATES - 2, wpa, wpb, wout, ln1_w, ln1_b, 256)
    y_p = _mlp(h_p, wup, wdn, ln2_w, ln2_b, 1024, 512)
    k_p = kt_p.reshape(nb, HEADS, 2, DA_HEAD_DIM, seq).transpose(0, 4, 1, 2, 3)[None]
    v_p = vt_p.reshape(1, nb, seq, HEADS, HEAD_W)

    xs = x_sample.reshape(db * tnew, D_MODEL)
    proj_s = _in_proj(xs.astype(BF16), w_in, db * tnew, skip_kv=False)
    o_hg_s, s_s = _hgrn(proj_s, BLK_HQ, hg_lb, hg_norm_w,
                        state_hgrn.reshape(db, HEADS, HEAD_W, HEAD_W), db, tnew, 4, tnew, tnew)
    o_da_s = _attn_sample(proj_s, cache_k, cache_v, page_table, *lam_args, db, tnew, 16)
    h_s = _merge(xs, o_hg_s, o_da_s, proj_s, BLK_GATES, wpa, wpb, wout, ln1_w, ln1_b, 256)
    y_s = _mlp(h_s, wup, wdn, ln2_w, ln2_b, 256, 512)
    k_s = proj_s[:, BLK_K * BRANCH_W:(BLK_K + 1) * BRANCH_W].reshape(1, db, tnew, HEADS, 2, DA_HEAD_DIM)
    v_s = proj_s[:, BLK_V * BRANCH_W:(BLK_V + 1) * BRANCH_W].reshape(1, db, tnew, HEADS, HEAD_W)

    return (y_p.reshape(nb, seq, D_MODEL), y_s.reshape(db, tnew, D_MODEL),
            k_p, v_p, s_p[None], k_s, v_s, s_s[None])
```

```python
import functools
import math

import jax
import jax.numpy as jnp
from jax import lax
from jax.experimental import pallas as pl
from jax.experimental.pallas import tpu as pltpu

D_MODEL = 2048
HEADS = 8
HEAD_W = 128
DA_HEAD_DIM = 64
BRANCH_W = HEADS * HEAD_W
D_FF = 4 * D_MODEL
PAGE = 128
LN_EPS = 1e-5
RMS_EPS = 1e-6
DEPTH = 1
ALPHA = (2 * DEPTH) ** 0.25
LAM_INIT = 0.8 - 0.6 * math.exp(-0.3 * 0)
IN_WIDTH = 7 * BRANCH_W + 2 * D_MODEL
BLK_Q, BLK_K, BLK_V, BLK_HQ, BLK_GATES = 0, 1, 2, 3, 7
PACKED_HQ, PACKED_GATES = BLK_HQ - 2, BLK_GATES - 2
HGRN_GROUP = 8
NEG = -1e30

F32 = jnp.float32
BF16 = jnp.bfloat16
MIB = 1024 * 1024
VMEM_LIMIT = 52 * MIB


def _cparams(sem, vmem_limit=VMEM_LIMIT):
    return pltpu.CompilerParams(dimension_semantics=sem, vmem_limit_bytes=vmem_limit)


def _sigmoid(x):
    return 1.0 / (1.0 + jnp.exp(-x))


def _proj_kernel(xp_ref, xs_ref, w_ref, op_ref, os_ref, wb, *, n_ptiles):
    i = pl.program_id(1)

    @pl.when(i == 0)
    def _():
        wb[...] = w_ref[...].astype(BF16)

    @pl.when(i < n_ptiles)
    def _():
        op_ref[...] = jnp.dot(xp_ref[...], wb[...], preferred_element_type=F32)

    @pl.when(i == n_ptiles)
    def _():
        os_ref[...] = jnp.dot(xs_ref[...], wb[...], preferred_element_type=F32)


def _in_proj(xp_bf, xs_bf, w_in, tm):
    tp, k = xp_bf.shape
    ts = xs_bf.shape[0]
    n_pt = tp // tm
    tn = BRANCH_W
    n_blk = IN_WIDTH // tn - 2
    ptile = lambda i: jnp.minimum(i, n_pt - 1)
    return pl.pallas_call(
        functools.partial(_proj_kernel, n_ptiles=n_pt),
        grid=(n_blk, n_pt + 1),
        in_specs=[pl.BlockSpec((tm, k), lambda j, i: (ptile(i), 0)),
                  pl.BlockSpec((ts, k), lambda j, i: (0, 0)),
                  pl.BlockSpec((None, k, tn), lambda j, i: (0, 0, j + 2 * jnp.minimum(j, 1)))],
        out_specs=[pl.BlockSpec((tm, tn), lambda j, i: (ptile(i), j)),
                   pl.BlockSpec((ts, tn), lambda j, i: (0, j))],
        out_shape=[jax.ShapeDtypeStruct((tp, n_blk * tn), F32),
                   jax.ShapeDtypeStruct((ts, n_blk * tn), F32)],
        scratch_shapes=[pltpu.VMEM((k, tn), BF16)],
        compiler_params=_cparams(("arbitrary", "arbitrary")),
        name="in_proj",
    )(xp_bf, xs_bf, w_in)


def _proj_kv_kernel(xp_ref, xs_ref, wk_ref, wv_ref, v_ref, kt_ref, vt_ref, xpb_ref,
                    ks_ref, vs_ref, xsb_ref, wkb, wvb, *, n_ptiles):
    i = pl.program_id(0)

    @pl.when(i == 0)
    def _():
        wkb[...] = wk_ref[...].astype(BF16)
        wvb[...] = wv_ref[...].astype(BF16)

    @pl.when(i < n_ptiles)
    def _():
        x = xp_ref[...].astype(BF16)
        xpb_ref[...] = x
        tm = x.shape[0]
        kt_ref[...] = jnp.dot(x, wkb[...], preferred_element_type=F32).T
        v = jnp.dot(x, wvb[...], preferred_element_type=F32)
        v_ref[...] = v
        for h in range(HEADS):
            vt_ref[pl.ds(h, tm, stride=HEADS), :] = v[:, h * HEAD_W:(h + 1) * HEAD_W]

    @pl.when(i == n_ptiles)
    def _():
        x = xs_ref[...].astype(BF16)
        xsb_ref[...] = x
        ks_ref[...] = jnp.dot(x, wkb[...], preferred_element_type=F32)
        vs_ref[...] = jnp.dot(x, wvb[...], preferred_element_type=F32)


def _proj_kv(xp, xs, w_in, nb, seq, tm):
    tp, k = xp.shape
    ts = xs.shape[0]
    nt = seq // tm
    n_pt = tp // tm
    pt = lambda i: jnp.minimum(i, n_pt - 1)
    const = lambda i: (0, 0)
    wspec = lambda c: pl.BlockSpec((None, k, BRANCH_W), lambda i: (0, 0, c), pipeline_mode=pl.Buffered(1))
    return pl.pallas_call(
        functools.partial(_proj_kv_kernel, n_ptiles=n_pt),
        grid=(n_pt + 1,),
        in_specs=[pl.BlockSpec((tm, k), lambda i: (pt(i), 0)),
                  pl.BlockSpec((ts, k), const, pipeline_mode=pl.Buffered(1)),
                  wspec(BLK_K), wspec(BLK_V)],
        out_specs=[pl.BlockSpec((tm, BRANCH_W), lambda i: (pt(i), 0)),
                   pl.BlockSpec((None, BRANCH_W, tm), lambda i: (pt(i) // nt, 0, pt(i) % nt)),
                   pl.BlockSpec((tm * HEADS, HEAD_W), lambda i: (pt(i), 0)),
                   pl.BlockSpec((tm, k), lambda i: (pt(i), 0)),
                   pl.BlockSpec((ts, BRANCH_W), const),
                   pl.BlockSpec((ts, BRANCH_W), const),
                   pl.BlockSpec((ts, k), const)],
        out_shape=[jax.ShapeDtypeStruct((tp, BRANCH_W), F32),
                   jax.ShapeDtypeStruct((nb, BRANCH_W, seq), F32),
                   jax.ShapeDtypeStruct((tp * HEADS, HEAD_W), F32),
                   jax.ShapeDtypeStruct((tp, k), BF16),
                   jax.ShapeDtypeStruct((ts, BRANCH_W), F32),
                   jax.ShapeDtypeStruct((ts, BRANCH_W), F32),
                   jax.ShapeDtypeStruct((ts, k), BF16)],
        scratch_shapes=[pltpu.VMEM((k, BRANCH_W), BF16), pltpu.VMEM((k, BRANCH_W), BF16)],
        compiler_params=_cparams(("arbitrary",), vmem_limit=56 * MIB),
        name="proj_kv",
    )(xp, xs, w_in, w_in)


def _hgrn_kernel(*refs, nbb, tb, c, has_s0):
    if has_s0:
        hq_ref, hf_ref, hi_ref, hg_ref, lb_ref, nw_ref, s0_ref, o_ref, s_ref, st_scr = refs
    else:
        hq_ref, hf_ref, hi_ref, hg_ref, lb_ref, nw_ref, o_ref, s_ref, st_scr = refs
        s0_ref = None
    t = pl.program_id(1)

    @pl.when(t == 0)
    def _():
        for q in range(nbb):
            for h in range(HEADS):
                if has_s0:
                    st_scr[q * HEADS + h] = s0_ref[q, h].T
                else:
                    st_scr[q * HEADS + h] = jnp.zeros((HEAD_W, HEAD_W), F32)

    lbx = lb_ref[...]
    lbe = jnp.exp(lbx - jnp.max(lbx, axis=0, keepdims=True))
    lb_all = lbe[0:1, :] / jnp.sum(lbe, axis=0, keepdims=True)
    nw_t = jnp.concatenate([nw_ref[...]] * HEADS, axis=1)

    row = lax.broadcasted_iota(jnp.int32, (c, c), 0)
    col = lax.broadcasted_iota(jnp.int32, (c, c), 1)
    tri = jnp.where(row >= col, 1.0, 0.0).astype(BF16)
    causal = row >= col

    one_m_lb = 1.0 - lb_all

    def chunk_group(items):
        pre = []
        for r, _ in items:
            z = hf_ref[r, :]
            logf = jnp.log(lb_all + one_m_lb * _sigmoid(z))
            kk = one_m_lb * _sigmoid(-z)
            hq = hq_ref[r, :]
            qq = hq * _sigmoid(hq)
            v_bf = hi_ref[r, :].astype(BF16)
            p0 = logf.astype(BF16)
            r1 = logf - p0.astype(F32)
            p1 = r1.astype(BF16)
            p2 = (r1 - p1.astype(F32)).astype(BF16)
            b = (jnp.dot(tri, p0, preferred_element_type=F32)
                 + jnp.dot(tri, p1, preferred_element_type=F32)
                 + jnp.dot(tri, p2, preferred_element_type=F32))
            pre.append((b, qq, kk, v_bf))
        a = [[jnp.zeros((c, c), F32) for _ in range(HEADS)] for _ in items]
        for s in range(c):
            lo = (s // 8) * 8
            for n, (b, qq, kk, _) in enumerate(pre):
                d = jnp.minimum(b[lo:, :] - b[s:s + 1, :], 0.0)
                term = qq[lo:, :] * kk[s:s + 1, :] * jnp.exp(d)
                for h in range(HEADS):
                    colsum = jnp.sum(term[:, h * HEAD_W:(h + 1) * HEAD_W], axis=-1, keepdims=True)
                    if lo:
                        colsum = jnp.concatenate([jnp.zeros((lo, 1), F32), colsum], axis=0)
                    a[n][h] = jnp.where(col == s, colsum, a[n][h])
        for n, ((r, q), (b, qq, kk, v_bf)) in enumerate(zip(items, pre)):
            b_last = b[c - 1:c, :]
            decay = jnp.exp(b_last)
            qe = (qq * jnp.exp(b)).astype(BF16)
            kd = (kk * jnp.exp(b_last - b)).astype(BF16)
            gate = nw_t * _sigmoid(hg_ref[r, :])
            for h in range(HEADS):
                cs = slice(h * HEAD_W, (h + 1) * HEAD_W)
                i = q * HEADS + h
                a_h = jnp.where(causal, a[n][h], 0.0).astype(BF16)
                o = (jnp.dot(a_h, v_bf[:, cs], preferred_element_type=F32)
                     + lax.dot_general(qe[:, cs], st_scr[i].astype(BF16), (((1,), (1,)), ((), ())),
                                       preferred_element_type=F32))
                u_t = lax.dot_general(v_bf[:, cs], kd[:, cs], (((0,), (0,)), ((), ())),
                                      preferred_element_type=F32)
                st_scr[i] = st_scr[i] * decay[:, cs] + u_t
                rms = lax.rsqrt(jnp.mean(o * o, axis=-1, keepdims=True) + RMS_EPS)
                o_ref[r, cs] = o * rms * gate[:, cs]

    n_chunks = tb // c
    if n_chunks == 1:
        chunk_group([(pl.ds(q * tb, c), q) for q in range(nbb)])
    else:
        group = math.gcd(n_chunks, HGRN_GROUP)
        assert nbb == 1

        def trip(gi, carry):
            base = gi * (group * c)
            chunk_group([(pl.ds(pl.multiple_of(base + n * c, c), c), 0) for n in range(group)])
            return carry

        lax.fori_loop(0, n_chunks // group, trip, 0)

    @pl.when(t == pl.num_programs(1) - 1)
    def _():
        for q in range(nbb):
            for h in range(HEADS):
                s_ref[q, h] = st_scr[q * HEADS + h].T


def _hgrn(proj, blk_hq, hg_lb, hg_norm_w, s0, nb, seq, nbb, tb, c):
    nt = seq // tb
    assert nbb == 1 or nt == 1
    has_s0 = s0 is not None

    def tok(colblk):
        return pl.BlockSpec((nbb * tb, BRANCH_W), lambda b, t: (b * nt + t, colblk))

    state_spec = pl.BlockSpec((nbb, HEADS, HEAD_W, HEAD_W), lambda b, t: (b, 0, 0, 0))
    in_specs = [tok(blk_hq), tok(blk_hq + 1), tok(blk_hq + 2), tok(blk_hq + 3),
                pl.BlockSpec(hg_lb.shape, lambda b, t: (0, 0)),
                pl.BlockSpec((1, HEAD_W), lambda b, t: (0, 0))]
    args = [proj, proj, proj, proj, hg_lb, hg_norm_w]
    if has_s0:
        in_specs.append(state_spec)
        args.append(s0)
    return pl.pallas_call(
        functools.partial(_hgrn_kernel, nbb=nbb, tb=tb, c=c, has_s0=has_s0),
        grid=(nb // nbb, nt),
        in_specs=in_specs,
        out_specs=[pl.BlockSpec((nbb * tb, BRANCH_W), lambda b, t: (b * nt + t, 0)), state_spec],
        out_shape=[jax.ShapeDtypeStruct((nb * seq, BRANCH_W), F32),
                   jax.ShapeDtypeStruct((nb, HEADS, HEAD_W, HEAD_W), F32)],
        scratch_shapes=[pltpu.VMEM((nbb * HEADS, HEAD_W, HEAD_W), F32)],
        compiler_params=_cparams(("parallel", "arbitrary")),
        name="hgrn_s0" if has_s0 else "hgrn",
    )(*args)


def _lam(q1_ref, k1_ref, q2_ref, k2_ref):
    s1 = jnp.sum(q1_ref[...] * k1_ref[...], axis=-1, keepdims=True)
    s2 = jnp.sum(q2_ref[...] * k2_ref[...], axis=-1, keepdims=True)
    return jnp.exp(s1) - jnp.exp(s2) + LAM_INIT


def _da_finish(o1, o2, lam, nw):
    o = o1 - lam * o2
    rms = lax.rsqrt(jnp.mean(o * o, axis=-1, keepdims=True) + RMS_EPS)
    return o * rms * nw * (1.0 - LAM_INIT)


def _attn_prompt_kernel(q_ref, k_ref, v_ref, lq1, lk1, lq2, lk2, nw_ref, o_ref, kb, vb, *, tq, tk):
    qi = pl.program_id(2)

    @pl.when(qi == 0)
    def _():
        kb[...] = k_ref[...].astype(BF16)
        vb[...] = v_ref[...].astype(BF16)

    q = q_ref[...] * (DA_HEAD_DIM ** -0.5)
    lane = lax.broadcasted_iota(jnp.int32, q.shape, 1)
    qa = jnp.where(lane < DA_HEAD_DIM, q, 0.0).astype(BF16)
    qb = jnp.where(lane >= DA_HEAD_DIM, q, 0.0).astype(BF16)
    row0 = lax.broadcasted_iota(jnp.int32, (tq, tk), 0)
    col0 = lax.broadcasted_iota(jnp.int32, (tq, tk), 1)

    def body(j, carry, masked=False):
        m1, l1, a1, m2, l2, a2 = carry
        ks = pl.ds(pl.multiple_of(j * tk, tk), tk)
        kj = kb[:, ks]
        vj = vb[ks, :]
        outs = []
        for qm, m, l, a in ((qa, m1, l1, a1), (qb, m2, l2, a2)):
            s = jnp.dot(qm, kj, preferred_element_type=F32)
            if masked:
                s = jnp.where(col0 <= row0, s, NEG)
            m_new = jnp.maximum(m, jnp.max(s, axis=-1, keepdims=True))
            corr = jnp.exp(m - m_new)
            p = jnp.exp(s - m_new)
            l_new = l * corr + jnp.sum(p, axis=-1, keepdims=True)
            a_new = a * corr + jnp.dot(p.astype(BF16), vj, preferred_element_type=F32)
            outs += [m_new, l_new, a_new]
        return tuple(outs)

    init = (jnp.full((tq, 1), NEG, F32), jnp.zeros((tq, 1), F32), jnp.zeros((tq, HEAD_W), F32)) * 2
    carry = lax.fori_loop(0, qi, body, init)
    m1, l1, a1, m2, l2, a2 = body(qi, carry, masked=True)
    lam = _lam(lq1, lk1, lq2, lk2)
    o_ref[...] = _da_finish(a1 / l1, a2 / l2, lam, nw_ref[...])


def _attn_prompt(proj, kt, v, lam_q1, lam_k1, lam_q2, lam_k2, da_norm_w, nb, seq, tq, tk):
    assert tq == tk
    nq = seq // tq
    small = lambda shp: pl.BlockSpec(shp, lambda b, h, i: (0, 0))
    return pl.pallas_call(
        functools.partial(_attn_prompt_kernel, tq=tq, tk=tk),
        grid=(nb, HEADS, nq),
        in_specs=[pl.BlockSpec((tq, HEAD_W), lambda b, h, i: (b * nq + i, BLK_Q * HEADS + h)),
                  pl.BlockSpec((None, HEAD_W, seq), lambda b, h, i: (b, h, 0)),
                  pl.BlockSpec((seq, HEAD_W), lambda b, h, i: (b, h)),
                  small((1, DA_HEAD_DIM)), small((1, DA_HEAD_DIM)),
                  small((1, DA_HEAD_DIM)), small((1, DA_HEAD_DIM)),
                  small((1, HEAD_W))],
        out_specs=pl.BlockSpec((tq, HEAD_W), lambda b, h, i: (b * nq + i, h)),
        out_shape=jax.ShapeDtypeStruct((nb * seq, BRANCH_W), F32),
        scratch_shapes=[pltpu.VMEM((HEAD_W, seq), BF16), pltpu.VMEM((seq, HEAD_W), BF16)],
        compiler_params=_cparams(("parallel", "parallel", "arbitrary")),
        name="attn_prompt",
    )(proj, kt, v, lam_q1, lam_k1, lam_q2, lam_k2, da_norm_w)


def _attn_sample_kernel(pt_ref, q_ref, kn_ref, vn_ref, *rest, pp, tnew):
    k_refs = rest[:pp]
    v_refs = rest[pp:2 * pp]
    lq1, lk1, lq2, lk2, nw_ref, o_ref, qblk, kcat, vcat, m_scr, l_scr, acc = rest[2 * pp:]
    g = pl.program_id(1)
    rows = HEADS * 2 * tnew

    @pl.when(g == 0)
    def _():
        q = q_ref[...] * (DA_HEAD_DIM ** -0.5)
        tiled = jnp.concatenate([q] * (HEADS * 2), axis=0)
        r = lax.broadcasted_iota(jnp.int32, tiled.shape, 0)
        cidx = lax.broadcasted_iota(jnp.int32, tiled.shape, 1)
        qblk[...] = jnp.where((r // tnew) == (cidx // DA_HEAD_DIM), tiled, 0.0).astype(BF16)
        m_scr[...] = jnp.full_like(m_scr, NEG)
        l_scr[...] = jnp.zeros_like(l_scr)
        acc[...] = jnp.zeros_like(acc)

    def update(s, kv_len):
        m_old = m_scr[...]
        m_new = jnp.maximum(m_old, jnp.max(s, axis=-1, keepdims=True))
        corr = jnp.exp(m_old - m_new)
        p = jnp.exp(s - m_new)
        l_scr[...] = l_scr[...] * corr + jnp.sum(p, axis=-1, keepdims=True)
        m_scr[...] = m_new
        p_bf = p.astype(BF16)
        for h in range(HEADS):
            rs = slice(h * 2 * tnew, (h + 1) * 2 * tnew)
            pv = jnp.dot(p_bf[rs, :], vcat[h, 0:kv_len, :], preferred_element_type=F32)
            acc[rs, :] = acc[rs, :] * corr[rs, :] + pv

    for j in range(pp):
        ls = slice(j * PAGE, (j + 1) * PAGE)
        kcat[:, ls] = k_refs[j][...].astype(BF16)
        for h in range(HEADS):
            vcat[h, ls, :] = v_refs[j][pl.ds(h, PAGE, stride=HEADS), :].astype(BF16)
    s = jnp.dot(qblk[...], kcat[...], preferred_element_type=F32)
    update(s, pp * PAGE)

    @pl.when(g == pl.num_programs(1) - 1)
    def _():
        pad = jnp.zeros((PAGE - tnew, BRANCH_W), F32)
        kn = jnp.concatenate([kn_ref[...], pad], axis=0).astype(BF16)
        vn = jnp.concatenate([vn_ref[...], pad], axis=0).astype(BF16)
        for h in range(HEADS):
            vcat[h, 0:PAGE, :] = vn[:, h * HEAD_W:(h + 1) * HEAD_W]
        sn = lax.dot_general(qblk[...], kn, (((1,), (1,)), ((), ())),
                             preferred_element_type=F32)
        r = lax.broadcasted_iota(jnp.int32, sn.shape, 0)
        cidx = lax.broadcasted_iota(jnp.int32, sn.shape, 1)
        sn = jnp.where(cidx <= (r % tnew), sn, NEG)
        update(sn, PAGE)
        lam = _lam(lq1, lk1, lq2, lk2)
        o = acc[...] / l_scr[...]
        nw = nw_ref[...]
        for h in range(HEADS):
            o1 = o[h * 2 * tnew:h * 2 * tnew + tnew, :]
            o2 = o[h * 2 * tnew + tnew:(h + 1) * 2 * tnew, :]
            o_ref[:, h * HEAD_W:(h + 1) * HEAD_W] = _da_finish(o1, o2, lam, nw)


def _attn_sample(proj, k_new, v_new, cache_k, cache_v, page_table, lam_q1, lam_k1, lam_q2, lam_k2,
                 da_norm_w, nb, tnew, pp):
    n_pages = page_table.shape[1]
    ng = n_pages // pp
    ck = cache_k.reshape(-1, PAGE, BRANCH_W).transpose(0, 2, 1)
    cv = cache_v.reshape(-1, PAGE * HEADS, HEAD_W)
    rows = HEADS * 2 * tnew

    def page_spec(j):
        return pl.BlockSpec((None, BRANCH_W, PAGE), lambda b, g, pt: (pt[b, g * pp + j], 0, 0))

    small = lambda shp: pl.BlockSpec(shp, lambda b, g, pt: (0, 0))
    in_specs = ([pl.BlockSpec((tnew, BRANCH_W), lambda b, g, pt: (b, BLK_Q)),
                 pl.BlockSpec((tnew, BRANCH_W), lambda b, g, pt: (b, 0)),
                 pl.BlockSpec((tnew, BRANCH_W), lambda b, g, pt: (b, 0))]
                + [page_spec(j) for j in range(pp)] + [page_spec(j) for j in range(pp)]
                + [small((1, DA_HEAD_DIM))] * 4 + [small((1, HEAD_W))])
    args = [page_table, proj, k_new, v_new, *([ck] * pp), *([cv] * pp),
            lam_q1, lam_k1, lam_q2, lam_k2, da_norm_w]
    scratch = [pltpu.VMEM((rows, BRANCH_W), BF16),
               pltpu.VMEM((BRANCH_W, pp * PAGE), BF16),
               pltpu.VMEM((HEADS, pp * PAGE, HEAD_W), BF16),
               pltpu.VMEM((rows, 1), F32),
               pltpu.VMEM((rows, 1), F32),
               pltpu.VMEM((rows, HEAD_W), F32)]
    return pl.pallas_call(
        functools.partial(_attn_sample_kernel, pp=pp, tnew=tnew),
        grid_spec=pltpu.PrefetchScalarGridSpec(
            num_scalar_prefetch=1, grid=(nb, ng), in_specs=in_specs,
            out_specs=pl.BlockSpec((tnew, BRANCH_W), lambda b, g, pt: (b, 0)),
            scratch_shapes=scratch),
        out_shape=jax.ShapeDtypeStruct((nb * tnew, BRANCH_W), F32),
        compiler_params=_cparams(("parallel", "arbitrary")),
        name="attn_sample",
    )(*args)


def _layernorm(x, w, b):
    mu = jnp.mean(x, axis=-1, keepdims=True)
    xc = x - mu
    var = jnp.mean(xc * xc, axis=-1, keepdims=True)
    return xc * lax.rsqrt(var + LN_EPS) * w + b


def _merge_kernel(x_ref, ohg_ref, oda_ref, ga0_ref, ga1_ref, gb0_ref, gb1_ref, wpa_ref, wpb_ref,
                  wout_ref, lw_ref, lb_ref, h_ref):
    pa = jnp.dot(ohg_ref[...].astype(BF16), wpa_ref[...], preferred_element_type=F32)
    pb = jnp.dot(oda_ref[...].astype(BF16), wpb_ref[...], preferred_element_type=F32)
    half = BRANCH_W
    merged = jnp.concatenate(
        [_sigmoid(ga0_ref[...]) * pa[:, :half] + _sigmoid(gb0_ref[...]) * pb[:, :half],
         _sigmoid(ga1_ref[...]) * pa[:, half:] + _sigmoid(gb1_ref[...]) * pb[:, half:]], axis=1)
    y = ALPHA * x_ref[...] + jnp.dot(merged.astype(BF16), wout_ref[...], preferred_element_type=F32)
    h_ref[...] = _layernorm(y, lw_ref[...], lb_ref[...])


def _merge(x, o_hg, o_da, proj, blk_gates, wpa, wpb, wout, ln_w, ln_b, tm):
    t = x.shape[0]
    rowblk = lambda w: pl.BlockSpec((tm, w), lambda i: (i, 0))
    gate = lambda c: pl.BlockSpec((tm, BRANCH_W), lambda i: (i, blk_gates + c))
    const = lambda shp: pl.BlockSpec(shp, lambda i: (0, 0), pipeline_mode=pl.Buffered(1))
    return pl.pallas_call(
        _merge_kernel,
        grid=(t // tm,),
        in_specs=[rowblk(D_MODEL), rowblk(BRANCH_W), rowblk(BRANCH_W),
                  gate(0), gate(1), gate(2), gate(3),
                  const((BRANCH_W, D_MODEL)), const((BRANCH_W, D_MODEL)), const((D_MODEL, D_MODEL)),
                  const((1, D_MODEL)), const((1, D_MODEL))],
        out_specs=rowblk(D_MODEL),
        out_shape=jax.ShapeDtypeStruct((t, D_MODEL), F32),
        compiler_params=_cparams(("parallel",)),
        name="merge_out_ln",
    )(x, o_hg, o_da, proj, proj, proj, proj, wpa, wpb, wout, ln_w, ln_b)


def _mlp_init(h_ref, y_ref, hb):
    hb[...] = h_ref[...].astype(BF16)
    y_ref[...] = jnp.zeros_like(y_ref)


def _mlp_chunk(wup, wdn, y_ref, hb):
    u = jnp.dot(hb[...], wup, preferred_element_type=F32)
    u = jnp.maximum(u, 0.0)
    y_ref[...] += jnp.dot((u * u).astype(BF16), wdn, preferred_element_type=F32)


def _mlp_finish(h_ref, lw_ref, lb_ref, y_ref):
    y_ref[...] = _layernorm(ALPHA * h_ref[...] + y_ref[...], lw_ref[...], lb_ref[...])


def _mlp_kernel(hp_ref, hs_ref, wup_ref, wdn_ref, lw_ref, lb_ref, yp_ref, ys_ref, hpb, hsb, *, n_ptiles):
    i = pl.program_id(0)
    j = pl.program_id(1)
    last = pl.num_programs(1) - 1

    def tile(h_ref, y_ref, hb):
        pl.when(j == 0)(lambda: _mlp_init(h_ref, y_ref, hb))
        _mlp_chunk(wup_ref[...], wdn_ref[...], y_ref, hb)
        pl.when(j == last)(lambda: _mlp_finish(h_ref, lw_ref, lb_ref, y_ref))

    pl.when(i < n_ptiles)(lambda: tile(hp_ref, yp_ref, hpb))
    pl.when(i == n_ptiles)(lambda: tile(hs_ref, ys_ref, hsb))


def _mlp(h_p, h_s, w_up, w_down, ln_w, ln_b, tm, tf):
    tp, ts = h_p.shape[0], h_s.shape[0]
    n_pt = tp // tm
    ptile = lambda i, j: (jnp.minimum(i, n_pt - 1), 0)
    const = lambda i, j: (0, 0)
    return pl.pallas_call(
        functools.partial(_mlp_kernel, n_ptiles=n_pt),
        grid=(n_pt + 1, D_FF // tf),
        in_specs=[pl.BlockSpec((tm, D_MODEL), ptile, pipeline_mode=pl.Buffered(1)),
                  pl.BlockSpec((ts, D_MODEL), const, pipeline_mode=pl.Buffered(1)),
                  pl.BlockSpec((D_MODEL, tf), lambda i, j: (0, j)),
                  pl.BlockSpec((tf, D_MODEL), lambda i, j: (j, 0)),
                  pl.BlockSpec((1, D_MODEL), const),
                  pl.BlockSpec((1, D_MODEL), const)],
        out_specs=[pl.BlockSpec((tm, D_MODEL), ptile), pl.BlockSpec((ts, D_MODEL), const)],
        out_shape=[jax.ShapeDtypeStruct((tp, D_MODEL), F32), jax.ShapeDtypeStruct((ts, D_MODEL), F32)],
        scratch_shapes=[pltpu.VMEM((tm, D_MODEL), BF16), pltpu.VMEM((ts, D_MODEL), BF16)],
        compiler_params=_cparams(("arbitrary", "arbitrary")),
        name="mlp_ln",
    )(h_p, h_s, w_up, w_down, ln_w, ln_b)


def kernel(x_prompt, x_sample, cache_k, cache_v, state_hgrn, page_table, w_in, hg_lb, hg_norm_w,
           lam_q1, lam_k1, lam_q2, lam_k2, da_norm_w, w_pa, w_pb, w_out, ln1_w, ln1_b,
           w_up, w_down, ln2_w, ln2_b):
    nb, seq, _ = x_prompt.shape
    db, tnew, _ = x_sample.shape
    wpa = w_pa[0].astype(BF16)
    wpb = w_pb[0].astype(BF16)
    wout = w_out[0].astype(BF16)
    lam_args = (lam_q1, lam_k1, lam_q2, lam_k2, da_norm_w)

    xp = x_prompt.reshape(nb * seq, D_MODEL)
    xs = x_sample.reshape(db * tnew, D_MODEL)
    v_p2d, kt_p, vt_p, xp_bf, k_s2d, v_s2d, xs_bf = _proj_kv(xp, xs, w_in, nb, seq, 512)
    proj_p, proj_s = _in_proj(xp_bf, xs_bf, w_in, 1024)
    o_hg_p, s_p = _hgrn(proj_p, PACKED_HQ, hg_lb, hg_norm_w, None, nb, seq, 1, 256, 16)
    o_da_p = _attn_prompt(proj_p, kt_p, v_p2d, *lam_args, nb, seq, 1024, 1024)
    h_p = _merge(xp, o_hg_p, o_da_p, proj_p, PACKED_GATES, wpa, wpb, wout, ln1_w, ln1_b, 256)
    k_p = kt_p.reshape(nb, HEADS, 2, DA_HEAD_DIM, seq).transpose(0, 4, 1, 2, 3)[None]
    v_p = vt_p.reshape(1, nb, seq, HEADS, HEAD_W)

    o_hg_s, s_s = _hgrn(proj_s, PACKED_HQ, hg_lb, hg_norm_w,
                        state_hgrn.reshape(db, HEADS, HEAD_W, HEAD_W), db, tnew, 4, tnew, tnew)
    o_da_s = _attn_sample(proj_s, k_s2d, v_s2d, cache_k, cache_v, page_table, *lam_args, db, tnew, 16)
    h_s = _merge(xs, o_hg_s, o_da_s, proj_s, PACKED_GATES, wpa, wpb, wout, ln1_w, ln1_b, 256)
    y_p, y_s = _mlp(h_p, h_s, w_up[0].astype(BF16), w_down[0].astype(BF16), ln2_w, ln2_b, 1024, 512)
    k_s = k_s2d.reshape(1, db, tnew, HEADS, 2, DA_HEAD_DIM)
    v_s = v_s2d.reshape(1, db, tnew, HEADS, HEAD_W)

    return (y_p.reshape(nb, seq, D_MODEL), y_s.reshape(db, tnew, D_MODEL),
            k_p, v_p, s_p[None], k_s, v_s, s_s[None])
```

```python
import functools
import math

import jax
import jax.numpy as jnp
from jax import lax
from jax.experimental import pallas as pl
from jax.experimental.pallas import tpu as pltpu

D_MODEL = 2048
HEADS = 8
HEAD_W = 128
DA_HEAD_DIM = 64
BRANCH_W = HEADS * HEAD_W
D_FF = 4 * D_MODEL
PAGE = 128
LN_EPS = 1e-5
RMS_EPS = 1e-6
DEPTH = 1
ALPHA = (2 * DEPTH) ** 0.25
LAM_INIT = 0.8 - 0.6 * math.exp(-0.3 * 0)
IN_WIDTH = 7 * BRANCH_W + 2 * D_MODEL
BLK_Q, BLK_K, BLK_V, BLK_HQ, BLK_GATES = 0, 1, 2, 3, 7
PACKED_HQ, PACKED_GATES = BLK_HQ - 2, BLK_GATES - 2
HGRN_GROUP = 8
LOG2E = math.log2(math.e)
NEG = -1e30

F32 = jnp.float32
BF16 = jnp.bfloat16
MIB = 1024 * 1024
VMEM_LIMIT = 52 * MIB


def _cparams(sem, vmem_limit=VMEM_LIMIT):
    return pltpu.CompilerParams(dimension_semantics=sem, vmem_limit_bytes=vmem_limit)


def _sigmoid(x):
    return 1.0 / (1.0 + jnp.exp(-x))


def _proj_kernel(xp_ref, xs_ref, w_ref, op_ref, os_ref, wb, *, n_ptiles):
    i = pl.program_id(1)

    @pl.when(i == 0)
    def _():
        wb[...] = w_ref[...].astype(BF16)

    @pl.when(i < n_ptiles)
    def _():
        op_ref[...] = jnp.dot(xp_ref[...], wb[...], preferred_element_type=F32)

    @pl.when(i == n_ptiles)
    def _():
        os_ref[...] = jnp.dot(xs_ref[...], wb[...], preferred_element_type=F32)


def _in_proj(xp_bf, xs_bf, w_in, tm):
    tp, k = xp_bf.shape
    ts = xs_bf.shape[0]
    n_pt = tp // tm
    tn = BRANCH_W
    n_blk = IN_WIDTH // tn - 2
    ptile = lambda i: jnp.minimum(i, n_pt - 1)
    return pl.pallas_call(
        functools.partial(_proj_kernel, n_ptiles=n_pt),
        grid=(n_blk, n_pt + 1),
        in_specs=[pl.BlockSpec((tm, k), lambda j, i: (ptile(i), 0)),
                  pl.BlockSpec((ts, k), lambda j, i: (0, 0)),
                  pl.BlockSpec((None, k, tn), lambda j, i: (0, 0, j + 2 * jnp.minimum(j, 1)))],
        out_specs=[pl.BlockSpec((tm, tn), lambda j, i: (ptile(i), j)),
                   pl.BlockSpec((ts, tn), lambda j, i: (0, j))],
        out_shape=[jax.ShapeDtypeStruct((tp, n_blk * tn), F32),
                   jax.ShapeDtypeStruct((ts, n_blk * tn), F32)],
        scratch_shapes=[pltpu.VMEM((k, tn), BF16)],
        compiler_params=_cparams(("arbitrary", "arbitrary")),
        name="in_proj",
    )(xp_bf, xs_bf, w_in)


def _proj_kv_kernel(xp_ref, xs_ref, wk_ref, wv_ref, v_ref, kt_ref, vt_ref, xpb_ref,
                    ks_ref, vs_ref, xsb_ref, wkb, wvb, *, n_ptiles):
    i = pl.program_id(0)

    @pl.when(i == 0)
    def _():
        wkb[...] = wk_ref[...].astype(BF16)
        wvb[...] = wv_ref[...].astype(BF16)

    @pl.when(i < n_ptiles)
    def _():
        x = xp_ref[...].astype(BF16)
        xpb_ref[...] = x
        tm = x.shape[0]
        kt_ref[...] = jnp.dot(x, wkb[...], preferred_element_type=F32).T
        v = jnp.dot(x, wvb[...], preferred_element_type=F32)
        v_ref[...] = v
        for h in range(HEADS):
            vt_ref[pl.ds(h, tm, stride=HEADS), :] = v[:, h * HEAD_W:(h + 1) * HEAD_W]

    @pl.when(i == n_ptiles)
    def _():
        x = xs_ref[...].astype(BF16)
        xsb_ref[...] = x
        ks_ref[...] = jnp.dot(x, wkb[...], preferred_element_type=F32)
        vs_ref[...] = jnp.dot(x, wvb[...], preferred_element_type=F32)


def _proj_kv(xp, xs, w_in, nb, seq, tm):
    tp, k = xp.shape
    ts = xs.shape[0]
    nt = seq // tm
    n_pt = tp // tm
    pt = lambda i: jnp.minimum(i, n_pt - 1)
    const = lambda i: (0, 0)
    wspec = lambda c: pl.BlockSpec((None, k, BRANCH_W), lambda i: (0, 0, c), pipeline_mode=pl.Buffered(1))
    return pl.pallas_call(
        functools.partial(_proj_kv_kernel, n_ptiles=n_pt),
        grid=(n_pt + 1,),
        in_specs=[pl.BlockSpec((tm, k), lambda i: (pt(i), 0)),
                  pl.BlockSpec((ts, k), const, pipeline_mode=pl.Buffered(1)),
                  wspec(BLK_K), wspec(BLK_V)],
        out_specs=[pl.BlockSpec((tm, BRANCH_W), lambda i: (pt(i), 0)),
                   pl.BlockSpec((None, BRANCH_W, tm), lambda i: (pt(i) // nt, 0, pt(i) % nt)),
                   pl.BlockSpec((tm * HEADS, HEAD_W), lambda i: (pt(i), 0)),
                   pl.BlockSpec((tm, k), lambda i: (pt(i), 0)),
                   pl.BlockSpec((ts, BRANCH_W), const),
                   pl.BlockSpec((ts, BRANCH_W), const),
                   pl.BlockSpec((ts, k), const)],
        out_shape=[jax.ShapeDtypeStruct((tp, BRANCH_W), F32),
                   jax.ShapeDtypeStruct((nb, BRANCH_W, seq), F32),
                   jax.ShapeDtypeStruct((tp * HEADS, HEAD_W), F32),
                   jax.ShapeDtypeStruct((tp, k), BF16),
                   jax.ShapeDtypeStruct((ts, BRANCH_W), F32),
                   jax.ShapeDtypeStruct((ts, BRANCH_W), F32),
                   jax.ShapeDtypeStruct((ts, k), BF16)],
        scratch_shapes=[pltpu.VMEM((k, BRANCH_W), BF16), pltpu.VMEM((k, BRANCH_W), BF16)],
        compiler_params=_cparams(("arbitrary",), vmem_limit=56 * MIB),
        name="proj_kv",
    )(xp, xs, w_in, w_in)


def _hgrn_kernel(*refs, nbb, tb, c, has_s0):
    if has_s0:
        hq_ref, hf_ref, hi_ref, hg_ref, lb_ref, nw_ref, s0_ref, o_ref, s_ref, st_scr = refs
    else:
        hq_ref, hf_ref, hi_ref, hg_ref, lb_ref, nw_ref, o_ref, s_ref, st_scr = refs
        s0_ref = None
    t = pl.program_id(1)

    @pl.when(t == 0)
    def _():
        for q in range(nbb):
            for h in range(HEADS):
                if has_s0:
                    st_scr[q * HEADS + h] = s0_ref[q, h].T
                else:
                    st_scr[q * HEADS + h] = jnp.zeros((HEAD_W, HEAD_W), F32)

    lbx = lb_ref[...]
    lbe = jnp.exp(lbx - jnp.max(lbx, axis=0, keepdims=True))
    lb_all = lbe[0:1, :] / jnp.sum(lbe, axis=0, keepdims=True)
    nw_t = jnp.concatenate([nw_ref[...]] * HEADS, axis=1)

    row = lax.broadcasted_iota(jnp.int32, (c, c), 0)
    col = lax.broadcasted_iota(jnp.int32, (c, c), 1)
    tri = jnp.where(row >= col, 1.0, 0.0).astype(BF16)
    causal = row >= col

    one_m_lb = 1.0 - lb_all

    def chunk_group(items):
        pre = []
        for r, _ in items:
            z = hf_ref[r, :]
            logf = jnp.log2(lb_all + one_m_lb * _sigmoid(z))
            kk = one_m_lb * _sigmoid(-z)
            hq = hq_ref[r, :]
            qq = hq * _sigmoid(hq)
            v_bf = hi_ref[r, :].astype(BF16)
            p0 = logf.astype(BF16)
            r1 = logf - p0.astype(F32)
            p1 = r1.astype(BF16)
            p2 = (r1 - p1.astype(F32)).astype(BF16)
            b = (jnp.dot(tri, p0, preferred_element_type=F32)
                 + jnp.dot(tri, p1, preferred_element_type=F32)
                 + jnp.dot(tri, p2, preferred_element_type=F32))
            pre.append((b, qq, kk, v_bf))
        a = [[jnp.zeros((c, c), F32) for _ in range(HEADS)] for _ in items]
        for s in range(c):
            lo = (s // 8) * 8
            for n, (b, qq, kk, _) in enumerate(pre):
                d = jnp.minimum(b[lo:, :] - b[s:s + 1, :], 0.0)
                term = qq[lo:, :] * kk[s:s + 1, :] * jnp.exp2(d)
                for h in range(HEADS):
                    colsum = jnp.sum(term[:, h * HEAD_W:(h + 1) * HEAD_W], axis=-1, keepdims=True)
                    if lo:
                        colsum = jnp.concatenate([jnp.zeros((lo, 1), F32), colsum], axis=0)
                    a[n][h] = jnp.where(col == s, colsum, a[n][h])
        for n, ((r, q), (b, qq, kk, v_bf)) in enumerate(zip(items, pre)):
            b_last = b[c - 1:c, :]
            decay = jnp.exp2(b_last)
            qe = (qq * jnp.exp2(b)).astype(BF16)
            kd = (kk * jnp.exp2(b_last - b)).astype(BF16)
            gate = nw_t * _sigmoid(hg_ref[r, :])
            for h in range(HEADS):
                cs = slice(h * HEAD_W, (h + 1) * HEAD_W)
                i = q * HEADS + h
                a_h = jnp.where(causal, a[n][h], 0.0).astype(BF16)
                o = (jnp.dot(a_h, v_bf[:, cs], preferred_element_type=F32)
                     + lax.dot_general(qe[:, cs], st_scr[i].astype(BF16), (((1,), (1,)), ((), ())),
                                       preferred_element_type=F32))
                u_t = lax.dot_general(v_bf[:, cs], kd[:, cs], (((0,), (0,)), ((), ())),
                                      preferred_element_type=F32)
                st_scr[i] = st_scr[i] * decay[:, cs] + u_t
                rms = lax.rsqrt(jnp.mean(o * o, axis=-1, keepdims=True) + RMS_EPS)
                o_ref[r, cs] = o * rms * gate[:, cs]

    n_chunks = tb // c
    if n_chunks == 1:
        chunk_group([(pl.ds(q * tb, c), q) for q in range(nbb)])
    else:
        group = math.gcd(n_chunks, HGRN_GROUP)
        assert nbb == 1

        def trip(gi, carry):
            base = gi * (group * c)
            chunk_group([(pl.ds(pl.multiple_of(base + n * c, c), c), 0) for n in range(group)])
            return carry

        lax.fori_loop(0, n_chunks // group, trip, 0)

    @pl.when(t == pl.num_programs(1) - 1)
    def _():
        for q in range(nbb):
            for h in range(HEADS):
                s_ref[q, h] = st_scr[q * HEADS + h].T


def _hgrn(proj, blk_hq, hg_lb, hg_norm_w, s0, nb, seq, nbb, tb, c):
    nt = seq // tb
    assert nbb == 1 or nt == 1
    has_s0 = s0 is not None

    def tok(colblk):
        return pl.BlockSpec((nbb * tb, BRANCH_W), lambda b, t: (b * nt + t, colblk))

    state_spec = pl.BlockSpec((nbb, HEADS, HEAD_W, HEAD_W), lambda b, t: (b, 0, 0, 0))
    in_specs = [tok(blk_hq), tok(blk_hq + 1), tok(blk_hq + 2), tok(blk_hq + 3),
                pl.BlockSpec(hg_lb.shape, lambda b, t: (0, 0)),
                pl.BlockSpec((1, HEAD_W), lambda b, t: (0, 0))]
    args = [proj, proj, proj, proj, hg_lb, hg_norm_w]
    if has_s0:
        in_specs.append(state_spec)
        args.append(s0)
    return pl.pallas_call(
        functools.partial(_hgrn_kernel, nbb=nbb, tb=tb, c=c, has_s0=has_s0),
        grid=(nb // nbb, nt),
        in_specs=in_specs,
        out_specs=[pl.BlockSpec((nbb * tb, BRANCH_W), lambda b, t: (b * nt + t, 0)), state_spec],
        out_shape=[jax.ShapeDtypeStruct((nb * seq, BRANCH_W), F32),
                   jax.ShapeDtypeStruct((nb, HEADS, HEAD_W, HEAD_W), F32)],
        scratch_shapes=[pltpu.VMEM((nbb * HEADS, HEAD_W, HEAD_W), F32)],
        compiler_params=_cparams(("parallel", "arbitrary")),
        name="hgrn_s0" if has_s0 else "hgrn",
    )(*args)


def _lam(q1_ref, k1_ref, q2_ref, k2_ref):
    s1 = jnp.sum(q1_ref[...] * k1_ref[...], axis=-1, keepdims=True)
    s2 = jnp.sum(q2_ref[...] * k2_ref[...], axis=-1, keepdims=True)
    return jnp.exp(s1) - jnp.exp(s2) + LAM_INIT


def _da_finish(o1, o2, lam, nw):
    o = o1 - lam * o2
    rms = lax.rsqrt(jnp.mean(o * o, axis=-1, keepdims=True) + RMS_EPS)
    return o * rms * nw * (1.0 - LAM_INIT)


def _attn_prompt_kernel(q_ref, k_ref, v_ref, lq1, lk1, lq2, lk2, nw_ref, o_ref, kb, vb, *, tq, tk):
    qi = pl.program_id(2)

    @pl.when(qi == 0)
    def _():
        kb[...] = k_ref[...].astype(BF16)
        vb[...] = v_ref[...].astype(BF16)

    q = q_ref[...] * (DA_HEAD_DIM ** -0.5 * LOG2E)
    lane = lax.broadcasted_iota(jnp.int32, q.shape, 1)
    qa = jnp.where(lane < DA_HEAD_DIM, q, 0.0).astype(BF16)
    qb = jnp.where(lane >= DA_HEAD_DIM, q, 0.0).astype(BF16)
    row0 = lax.broadcasted_iota(jnp.int32, (tq, tk), 0)
    col0 = lax.broadcasted_iota(jnp.int32, (tq, tk), 1)

    def body(j, carry, masked=False):
        m1, l1, a1, m2, l2, a2 = carry
        ks = pl.ds(pl.multiple_of(j * tk, tk), tk)
        kj = kb[:, ks]
        vj = vb[ks, :]
        outs = []
        for qm, m, l, a in ((qa, m1, l1, a1), (qb, m2, l2, a2)):
            s = jnp.dot(qm, kj, preferred_element_type=F32)
            if masked:
                s = jnp.where(col0 <= row0, s, NEG)
            m_new = jnp.maximum(m, jnp.max(s, axis=-1, keepdims=True))
            corr = jnp.exp2(m - m_new)
            p = jnp.exp2(s - m_new)
            l_new = l * corr + jnp.sum(p, axis=-1, keepdims=True)
            a_new = a * corr + jnp.dot(p.astype(BF16), vj, preferred_element_type=F32)
            outs += [m_new, l_new, a_new]
        return tuple(outs)

    init = (jnp.full((tq, 1), NEG, F32), jnp.zeros((tq, 1), F32), jnp.zeros((tq, HEAD_W), F32)) * 2
    carry = lax.fori_loop(0, qi, body, init)
    m1, l1, a1, m2, l2, a2 = body(qi, carry, masked=True)
    lam = _lam(lq1, lk1, lq2, lk2)
    o_ref[...] = _da_finish(a1 / l1, a2 / l2, lam, nw_ref[...])


def _attn_prompt(proj, kt, v, lam_q1, lam_k1, lam_q2, lam_k2, da_norm_w, nb, seq, tq, tk):
    assert tq == tk
    nq = seq // tq
    small = lambda shp: pl.BlockSpec(shp, lambda b, h, i: (0, 0))
    return pl.pallas_call(
        functools.partial(_attn_prompt_kernel, tq=tq, tk=tk),
        grid=(nb, HEADS, nq),
        in_specs=[pl.BlockSpec((tq, HEAD_W), lambda b, h, i: (b * nq + i, BLK_Q * HEADS + h)),
                  pl.BlockSpec((None, HEAD_W, seq), lambda b, h, i: (b, h, 0)),
                  pl.BlockSpec((seq, HEAD_W), lambda b, h, i: (b, h)),
                  small((1, DA_HEAD_DIM)), small((1, DA_HEAD_DIM)),
                  small((1, DA_HEAD_DIM)), small((1, DA_HEAD_DIM)),
                  small((1, HEAD_W))],
        out_specs=pl.BlockSpec((tq, HEAD_W), lambda b, h, i: (b * nq + i, h)),
        out_shape=jax.ShapeDtypeStruct((nb * seq, BRANCH_W), F32),
        scratch_shapes=[pltpu.VMEM((HEAD_W, seq), BF16), pltpu.VMEM((seq, HEAD_W), BF16)],
        compiler_params=_cparams(("parallel", "parallel", "arbitrary")),
        name="attn_prompt",
    )(proj, kt, v, lam_q1, lam_k1, lam_q2, lam_k2, da_norm_w)


def _attn_sample_kernel(pt_ref, q_ref, kn_ref, vn_ref, *rest, pp, tnew):
    k_refs = rest[:pp]
    v_refs = rest[pp:2 * pp]
    lq1, lk1, lq2, lk2, nw_ref, o_ref, qblk, kcat, vcat, m_scr, l_scr, acc = rest[2 * pp:]
    g = pl.program_id(1)
    rows = HEADS * 2 * tnew

    @pl.when(g == 0)
    def _():
        q = q_ref[...] * (DA_HEAD_DIM ** -0.5)
        tiled = jnp.concatenate([q] * (HEADS * 2), axis=0)
        r = lax.broadcasted_iota(jnp.int32, tiled.shape, 0)
        cidx = lax.broadcasted_iota(jnp.int32, tiled.shape, 1)
        qblk[...] = jnp.where((r // tnew) == (cidx // DA_HEAD_DIM), tiled, 0.0).astype(BF16)
        m_scr[...] = jnp.full_like(m_scr, NEG)
        l_scr[...] = jnp.zeros_like(l_scr)
        acc[...] = jnp.zeros_like(acc)

    def update(s, kv_len):
        m_old = m_scr[...]
        m_new = jnp.maximum(m_old, jnp.max(s, axis=-1, keepdims=True))
        corr = jnp.exp(m_old - m_new)
        p = jnp.exp(s - m_new)
        l_scr[...] = l_scr[...] * corr + jnp.sum(p, axis=-1, keepdims=True)
        m_scr[...] = m_new
        p_bf = p.astype(BF16)
        for h in range(HEADS):
            rs = slice(h * 2 * tnew, (h + 1) * 2 * tnew)
            pv = jnp.dot(p_bf[rs, :], vcat[h, 0:kv_len, :], preferred_element_type=F32)
            acc[rs, :] = acc[rs, :] * corr[rs, :] + pv

    for j in range(pp):
        ls = slice(j * PAGE, (j + 1) * PAGE)
        kcat[:, ls] = k_refs[j][...].astype(BF16)
        for h in range(HEADS):
            vcat[h, ls, :] = v_refs[j][pl.ds(h, PAGE, stride=HEADS), :].astype(BF16)
    s = jnp.dot(qblk[...], kcat[...], preferred_element_type=F32)
    update(s, pp * PAGE)

    @pl.when(g == pl.num_programs(1) - 1)
    def _():
        pad = jnp.zeros((PAGE - tnew, BRANCH_W), F32)
        kn = jnp.concatenate([kn_ref[...], pad], axis=0).astype(BF16)
        vn = jnp.concatenate([vn_ref[...], pad], axis=0).astype(BF16)
        for h in range(HEADS):
            vcat[h, 0:PAGE, :] = vn[:, h * HEAD_W:(h + 1) * HEAD_W]
        sn = lax.dot_general(qblk[...], kn, (((1,), (1,)), ((), ())),
                             preferred_element_type=F32)
        r = lax.broadcasted_iota(jnp.int32, sn.shape, 0)
        cidx = lax.broadcasted_iota(jnp.int32, sn.shape, 1)
        sn = jnp.where(cidx <= (r % tnew), sn, NEG)
        update(sn, PAGE)
        lam = _lam(lq1, lk1, lq2, lk2)
        o = acc[...] / l_scr[...]
        nw = nw_ref[...]
        for h in range(HEADS):
            o1 = o[h * 2 * tnew:h * 2 * tnew + tnew, :]
            o2 = o[h * 2 * tnew + tnew:(h + 1) * 2 * tnew, :]
            o_ref[:, h * HEAD_W:(h + 1) * HEAD_W] = _da_finish(o1, o2, lam, nw)


def _attn_sample(proj, k_new, v_new, cache_k, cache_v, page_table, lam_q1, lam_k1, lam_q2, lam_k2,
                 da_norm_w, nb, tnew, pp):
    n_pages = page_table.shape[1]
    ng = n_pages // pp
    ck = cache_k.reshape(-1, PAGE, BRANCH_W).transpose(0, 2, 1)
    cv = cache_v.reshape(-1, PAGE * HEADS, HEAD_W)
    rows = HEADS * 2 * tnew

    def page_spec(j):
        return pl.BlockSpec((None, BRANCH_W, PAGE), lambda b, g, pt: (pt[b, g * pp + j], 0, 0))

    small = lambda shp: pl.BlockSpec(shp, lambda b, g, pt: (0, 0))
    in_specs = ([pl.BlockSpec((tnew, BRANCH_W), lambda b, g, pt: (b, BLK_Q)),
                 pl.BlockSpec((tnew, BRANCH_W), lambda b, g, pt: (b, 0)),
                 pl.BlockSpec((tnew, BRANCH_W), lambda b, g, pt: (b, 0))]
                + [page_spec(j) for j in range(pp)] + [page_spec(j) for j in range(pp)]
                + [small((1, DA_HEAD_DIM))] * 4 + [small((1, HEAD_W))])
    args = [page_table, proj, k_new, v_new, *([ck] * pp), *([cv] * pp),
            lam_q1, lam_k1, lam_q2, lam_k2, da_norm_w]
    scratch = [pltpu.VMEM((rows, BRANCH_W), BF16),
               pltpu.VMEM((BRANCH_W, pp * PAGE), BF16),
               pltpu.VMEM((HEADS, pp * PAGE, HEAD_W), BF16),
               pltpu.VMEM((rows, 1), F32),
               pltpu.VMEM((rows, 1), F32),
               pltpu.VMEM((rows, HEAD_W), F32)]
    return pl.pallas_call(
        functools.partial(_attn_sample_kernel, pp=pp, tnew=tnew),
        grid_spec=pltpu.PrefetchScalarGridSpec(
            num_scalar_prefetch=1, grid=(nb, ng), in_specs=in_specs,
            out_specs=pl.BlockSpec((tnew, BRANCH_W), lambda b, g, pt: (b, 0)),
            scratch_shapes=scratch),
        out_shape=jax.ShapeDtypeStruct((nb * tnew, BRANCH_W), F32),
        compiler_params=_cparams(("parallel", "arbitrary")),
        name="attn_sample",
    )(*args)


def _layernorm(x, w, b):
    mu = jnp.mean(x, axis=-1, keepdims=True)
    xc = x - mu
    var = jnp.mean(xc * xc, axis=-1, keepdims=True)
    return xc * lax.rsqrt(var + LN_EPS) * w + b


def _merge_kernel(x_ref, ohg_ref, oda_ref, ga0_ref, ga1_ref, gb0_ref, gb1_ref, wpa_ref, wpb_ref,
                  wout_ref, lw_ref, lb_ref, h_ref):
    pa = jnp.dot(ohg_ref[...].astype(BF16), wpa_ref[...], preferred_element_type=F32)
    pb = jnp.dot(oda_ref[...].astype(BF16), wpb_ref[...], preferred_element_type=F32)
    half = BRANCH_W
    merged = jnp.concatenate(
        [_sigmoid(ga0_ref[...]) * pa[:, :half] + _sigmoid(gb0_ref[...]) * pb[:, :half],
         _sigmoid(ga1_ref[...]) * pa[:, half:] + _sigmoid(gb1_ref[...]) * pb[:, half:]], axis=1)
    y = ALPHA * x_ref[...] + jnp.dot(merged.astype(BF16), wout_ref[...], preferred_element_type=F32)
    h_ref[...] = _layernorm(y, lw_ref[...], lb_ref[...])


def _merge(x, o_hg, o_da, proj, blk_gates, wpa, wpb, wout, ln_w, ln_b, tm):
    t = x.shape[0]
    rowblk = lambda w: pl.BlockSpec((tm, w), lambda i: (i, 0))
    gate = lambda c: pl.BlockSpec((tm, BRANCH_W), lambda i: (i, blk_gates + c))
    const = lambda shp: pl.BlockSpec(shp, lambda i: (0, 0), pipeline_mode=pl.Buffered(1))
    return pl.pallas_call(
        _merge_kernel,
        grid=(t // tm,),
        in_specs=[rowblk(D_MODEL), rowblk(BRANCH_W), rowblk(BRANCH_W),
                  gate(0), gate(1), gate(2), gate(3),
                  const((BRANCH_W, D_MODEL)), const((BRANCH_W, D_MODEL)), const((D_MODEL, D_MODEL)),
                  const((1, D_MODEL)), const((1, D_MODEL))],
        out_specs=rowblk(D_MODEL),
        out_shape=jax.ShapeDtypeStruct((t, D_MODEL), F32),
        compiler_params=_cparams(("parallel",)),
        name="merge_out_ln",
    )(x, o_hg, o_da, proj, proj, proj, proj, wpa, wpb, wout, ln_w, ln_b)


def _mlp_init(h_ref, y_ref, hb):
    hb[...] = h_ref[...].astype(BF16)
    y_ref[...] = jnp.zeros_like(y_ref)


def _mlp_chunk(wup, wdn, y_ref, hb):
    u = jnp.dot(hb[...], wup, preferred_element_type=F32)
    u = jnp.maximum(u, 0.0)
    y_ref[...] += jnp.dot((u * u).astype(BF16), wdn, preferred_element_type=F32)


def _mlp_finish(h_ref, lw_ref, lb_ref, y_ref):
    y_ref[...] = _layernorm(ALPHA * h_ref[...] + y_ref[...], lw_ref[...], lb_ref[...])


def _mlp_kernel(hp_ref, hs_ref, wup_ref, wdn_ref, lw_ref, lb_ref, yp_ref, ys_ref, hpb, hsb, *, n_ptiles):
    i = pl.program_id(0)
    j = pl.program_id(1)
    last = pl.num_programs(1) - 1

    def tile(h_ref, y_ref, hb):
        pl.when(j == 0)(lambda: _mlp_init(h_ref, y_ref, hb))
        _mlp_chunk(wup_ref[...], wdn_ref[...], y_ref, hb)
        pl.when(j == last)(lambda: _mlp_finish(h_ref, lw_ref, lb_ref, y_ref))

    pl.when(i < n_ptiles)(lambda: tile(hp_ref, yp_ref, hpb))
    pl.when(i == n_ptiles)(lambda: tile(hs_ref, ys_ref, hsb))


def _mlp(h_p, h_s, w_up, w_down, ln_w, ln_b, tm, tf):
    tp, ts = h_p.shape[0], h_s.shape[0]
    n_pt = tp // tm
    ptile = lambda i, j: (jnp.minimum(i, n_pt - 1), 0)
    const = lambda i, j: (0, 0)
    return pl.pallas_call(
        functools.partial(_mlp_kernel, n_ptiles=n_pt),
        grid=(n_pt + 1, D_FF // tf),
        in_specs=[pl.BlockSpec((tm, D_MODEL), ptile, pipeline_mode=pl.Buffered(1)),
                  pl.BlockSpec((ts, D_MODEL), const, pipeline_mode=pl.Buffered(1)),
                  pl.BlockSpec((D_MODEL, tf), lambda i, j: (0, j)),
                  pl.BlockSpec((tf, D_MODEL), lambda i, j: (j, 0)),
                  pl.BlockSpec((1, D_MODEL), const),
                  pl.BlockSpec((1, D_MODEL), const)],
        out_specs=[pl.BlockSpec((tm, D_MODEL), ptile), pl.BlockSpec((ts, D_MODEL), const)],
        out_shape=[jax.ShapeDtypeStruct((tp, D_MODEL), F32), jax.ShapeDtypeStruct((ts, D_MODEL), F32)],
        scratch_shapes=[pltpu.VMEM((tm, D_MODEL), BF16), pltpu.VMEM((ts, D_MODEL), BF16)],
        compiler_params=_cparams(("arbitrary", "arbitrary")),
        name="mlp_ln",
    )(h_p, h_s, w_up, w_down, ln_w, ln_b)


def kernel(x_prompt, x_sample, cache_k, cache_v, state_hgrn, page_table, w_in, hg_lb, hg_norm_w,
           lam_q1, lam_k1, lam_q2, lam_k2, da_norm_w, w_pa, w_pb, w_out, ln1_w, ln1_b,
           w_up, w_down, ln2_w, ln2_b):
    nb, seq, _ = x_prompt.shape
    db, tnew, _ = x_sample.shape
    wpa = w_pa[0].astype(BF16)
    wpb = w_pb[0].astype(BF16)
    wout = w_out[0].astype(BF16)
    lam_args = (lam_q1, lam_k1, lam_q2, lam_k2, da_norm_w)

    xp = x_prompt.reshape(nb * seq, D_MODEL)
    xs = x_sample.reshape(db * tnew, D_MODEL)
    v_p2d, kt_p, vt_p, xp_bf, k_s2d, v_s2d, xs_bf = _proj_kv(xp, xs, w_in, nb, seq, 512)
    proj_p, proj_s = _in_proj(xp_bf, xs_bf, w_in, 1024)
    o_hg_p, s_p = _hgrn(proj_p, PACKED_HQ, hg_lb, hg_norm_w, None, nb, seq, 1, 256, 16)
    o_da_p = _attn_prompt(proj_p, kt_p, v_p2d, *lam_args, nb, seq, 1024, 1024)
    h_p = _merge(xp, o_hg_p, o_da_p, proj_p, PACKED_GATES, wpa, wpb, wout, ln1_w, ln1_b, 256)
    k_p = kt_p.reshape(nb, HEADS, 2, DA_HEAD_DIM, seq).transpose(0, 4, 1, 2, 3)[None]
    v_p = vt_p.reshape(1, nb, seq, HEADS, HEAD_W)

    o_hg_s, s_s = _hgrn(proj_s, PACKED_HQ, hg_lb, hg_norm_w,
                        state_hgrn.reshape(db, HEADS, HEAD_W, HEAD_W), db, tnew, 4, tnew, tnew)
    o_da_s = _attn_sample(proj_s, k_s2d, v_s2d, cache_k, cache_v, page_table, *lam_args, db, tnew, 16)
    h_s = _merge(xs, o_hg_s, o_da_s, proj_s, PACKED_GATES, wpa, wpb, wout, ln1_w, ln1_b, 256)
    y_p, y_s = _mlp(h_p, h_s, w_up[0].astype(BF16), w_down[0].astype(BF16), ln2_w, ln2_b, 512, 1024)
    k_s = k_s2d.reshape(1, db, tnew, HEADS, 2, DA_HEAD_DIM)
    v_s = v_s2d.reshape(1, db, tnew, HEADS, HEAD_W)

    return (y_p.reshape(nb, seq, D_MODEL), y_s.reshape(db, tnew, D_MODEL),
            k_p, v_p, s_p[None], k_s, v_s, s_s[None])
```

```python
import functools
import math

import jax
import jax.numpy as jnp
from jax import lax
from jax.experimental import pallas as pl
from jax.experimental.pallas import tpu as pltpu

D_MODEL = 2048
HEADS = 8
HEAD_W = 128
DA_HEAD_DIM = 64
BRANCH_W = HEADS * HEAD_W
D_FF = 4 * D_MODEL
PAGE = 128
LN_EPS = 1e-5
RMS_EPS = 1e-6
DEPTH = 1
ALPHA = (2 * DEPTH) ** 0.25
LAM_INIT = 0.8 - 0.6 * math.exp(-0.3 * 0)
IN_WIDTH = 7 * BRANCH_W + 2 * D_MODEL
BLK_Q, BLK_K, BLK_V, BLK_HQ, BLK_GATES = 0, 1, 2, 3, 7
PACKED_HQ, PACKED_GATES = BLK_HQ - 2, BLK_GATES - 2
HGRN_GROUP = 8
LOG2E = math.log2(math.e)
NEG = -1e30

F32 = jnp.float32
BF16 = jnp.bfloat16
MIB = 1024 * 1024
VMEM_LIMIT = 52 * MIB


def _cparams(sem, vmem_limit=VMEM_LIMIT):
    return pltpu.CompilerParams(dimension_semantics=sem, vmem_limit_bytes=vmem_limit)


def _sigmoid(x):
    return 1.0 / (1.0 + jnp.exp(-x))


def _proj_kernel(xp_ref, xs_ref, w_ref, op_ref, os_ref, wb, *, n_ptiles):
    i = pl.program_id(1)

    @pl.when(i == 0)
    def _():
        wb[...] = w_ref[...].astype(BF16)

    @pl.when(i < n_ptiles)
    def _():
        op_ref[...] = jnp.dot(xp_ref[...], wb[...], preferred_element_type=F32)

    @pl.when(i == n_ptiles)
    def _():
        os_ref[...] = jnp.dot(xs_ref[...], wb[...], preferred_element_type=F32)


def _in_proj(xp_bf, xs_bf, w_in, tm):
    tp, k = xp_bf.shape
    ts = xs_bf.shape[0]
    n_pt = tp // tm
    tn = BRANCH_W
    n_blk = IN_WIDTH // tn - 2
    ptile = lambda i: jnp.minimum(i, n_pt - 1)
    return pl.pallas_call(
        functools.partial(_proj_kernel, n_ptiles=n_pt),
        grid=(n_blk, n_pt + 1),
        in_specs=[pl.BlockSpec((tm, k), lambda j, i: (ptile(i), 0)),
                  pl.BlockSpec((ts, k), lambda j, i: (0, 0)),
                  pl.BlockSpec((None, k, tn), lambda j, i: (0, 0, j + 2 * jnp.minimum(j, 1)))],
        out_specs=[pl.BlockSpec((tm, tn), lambda j, i: (ptile(i), j)),
                   pl.BlockSpec((ts, tn), lambda j, i: (0, j))],
        out_shape=[jax.ShapeDtypeStruct((tp, n_blk * tn), F32),
                   jax.ShapeDtypeStruct((ts, n_blk * tn), F32)],
        scratch_shapes=[pltpu.VMEM((k, tn), BF16)],
        compiler_params=_cparams(("arbitrary", "arbitrary")),
        name="in_proj",
    )(xp_bf, xs_bf, w_in)


def _proj_kv_kernel(xp_ref, xs_ref, wk_ref, wv_ref, v_ref, kt_ref, vt_ref, xpb_ref,
                    ks_ref, vs_ref, xsb_ref, wkb, wvb, *, n_ptiles):
    i = pl.program_id(0)

    @pl.when(i == 0)
    def _():
        wkb[...] = wk_ref[...].astype(BF16)
        wvb[...] = wv_ref[...].astype(BF16)

    @pl.when(i < n_ptiles)
    def _():
        x = xp_ref[...].astype(BF16)
        xpb_ref[...] = x
        tm = x.shape[0]
        kt_ref[...] = jnp.dot(x, wkb[...], preferred_element_type=F32).T
        v = jnp.dot(x, wvb[...], preferred_element_type=F32)
        v_ref[...] = v
        for h in range(HEADS):
            vt_ref[pl.ds(h, tm, stride=HEADS), :] = v[:, h * HEAD_W:(h + 1) * HEAD_W]

    @pl.when(i == n_ptiles)
    def _():
        x = xs_ref[...].astype(BF16)
        xsb_ref[...] = x
        ks_ref[...] = jnp.dot(x, wkb[...], preferred_element_type=F32)
        vs_ref[...] = jnp.dot(x, wvb[...], preferred_element_type=F32)


def _proj_kv(xp, xs, w_in, nb, seq, tm):
    tp, k = xp.shape
    ts = xs.shape[0]
    nt = seq // tm
    n_pt = tp // tm
    pt = lambda i: jnp.minimum(i, n_pt - 1)
    const = lambda i: (0, 0)
    wspec = lambda c: pl.BlockSpec((None, k, BRANCH_W), lambda i: (0, 0, c), pipeline_mode=pl.Buffered(1))
    return pl.pallas_call(
        functools.partial(_proj_kv_kernel, n_ptiles=n_pt),
        grid=(n_pt + 1,),
        in_specs=[pl.BlockSpec((tm, k), lambda i: (pt(i), 0)),
                  pl.BlockSpec((ts, k), const, pipeline_mode=pl.Buffered(1)),
                  wspec(BLK_K), wspec(BLK_V)],
        out_specs=[pl.BlockSpec((tm, BRANCH_W), lambda i: (pt(i), 0)),
                   pl.BlockSpec((None, BRANCH_W, tm), lambda i: (pt(i) // nt, 0, pt(i) % nt)),
                   pl.BlockSpec((tm * HEADS, HEAD_W), lambda i: (pt(i), 0)),
                   pl.BlockSpec((tm, k), lambda i: (pt(i), 0)),
                   pl.BlockSpec((ts, BRANCH_W), const),
                   pl.BlockSpec((ts, BRANCH_W), const),
                   pl.BlockSpec((ts, k), const)],
        out_shape=[jax.ShapeDtypeStruct((tp, BRANCH_W), F32),
                   jax.ShapeDtypeStruct((nb, BRANCH_W, seq), F32),
                   jax.ShapeDtypeStruct((tp * HEADS, HEAD_W), F32),
                   jax.ShapeDtypeStruct((tp, k), BF16),
                   jax.ShapeDtypeStruct((ts, BRANCH_W), F32),
                   jax.ShapeDtypeStruct((ts, BRANCH_W), F32),
                   jax.ShapeDtypeStruct((ts, k), BF16)],
        scratch_shapes=[pltpu.VMEM((k, BRANCH_W), BF16), pltpu.VMEM((k, BRANCH_W), BF16)],
        compiler_params=_cparams(("arbitrary",), vmem_limit=56 * MIB),
        name="proj_kv",
    )(xp, xs, w_in, w_in)


def _hgrn_kernel(*refs, nbb, tb, c, has_s0):
    if has_s0:
        hq_ref, hf_ref, hi_ref, hg_ref, lb_ref, nw_ref, s0_ref, o_ref, s_ref, st_scr = refs
    else:
        hq_ref, hf_ref, hi_ref, hg_ref, lb_ref, nw_ref, o_ref, s_ref, st_scr = refs
        s0_ref = None
    t = pl.program_id(1)

    @pl.when(t == 0)
    def _():
        for q in range(nbb):
            for h in range(HEADS):
                if has_s0:
                    st_scr[q * HEADS + h] = s0_ref[q, h].T
                else:
                    st_scr[q * HEADS + h] = jnp.zeros((HEAD_W, HEAD_W), F32)

    lbx = lb_ref[...]
    lbe = jnp.exp(lbx - jnp.max(lbx, axis=0, keepdims=True))
    lb_all = lbe[0:1, :] / jnp.sum(lbe, axis=0, keepdims=True)
    nw_t = jnp.concatenate([nw_ref[...]] * HEADS, axis=1)

    row = lax.broadcasted_iota(jnp.int32, (c, c), 0)
    col = lax.broadcasted_iota(jnp.int32, (c, c), 1)
    tri = jnp.where(row >= col, 1.0, 0.0).astype(BF16)
    causal = row >= col

    one_m_lb = 1.0 - lb_all

    def chunk_group(items):
        pre = []
        for r, _ in items:
            z = hf_ref[r, :]
            logf = jnp.log2(lb_all + one_m_lb * _sigmoid(z))
            kk = one_m_lb * _sigmoid(-z)
            hq = hq_ref[r, :]
            qq = hq * _sigmoid(hq)
            v_bf = hi_ref[r, :].astype(BF16)
            p0 = logf.astype(BF16)
            r1 = logf - p0.astype(F32)
            p1 = r1.astype(BF16)
            p2 = (r1 - p1.astype(F32)).astype(BF16)
            b = (jnp.dot(tri, p0, preferred_element_type=F32)
                 + jnp.dot(tri, p1, preferred_element_type=F32)
                 + jnp.dot(tri, p2, preferred_element_type=F32))
            pre.append((b, qq, kk, v_bf))
        nband = c // 8
        a = [[[jnp.zeros((8, c), F32) for _ in range(nband)] for _ in range(HEADS)] for _ in items]
        for s in range(c):
            i, lo = s // 8, (s // 8) * 8
            for n, (b, qq, kk, _) in enumerate(pre):
                d = jnp.minimum(b[lo:lo + 8, :] - b[s:s + 1, :], 0.0)
                term = qq[lo:lo + 8, :] * kk[s:s + 1, :] * jnp.exp2(d)
                for h in range(HEADS):
                    colsum = jnp.sum(term[:, h * HEAD_W:(h + 1) * HEAD_W], axis=-1, keepdims=True)
                    a[n][h][i] = jnp.where(col[:8] == s, colsum, a[n][h][i])
        for n, (b, qq, kk, _) in enumerate(pre):
            for i in range(1, nband):
                ref = b[8 * i:8 * i + 1, :]
                qt = (qq[8 * i:8 * i + 8, :] * jnp.exp2(b[8 * i:8 * i + 8, :] - ref)).astype(BF16)
                kt = jnp.where(row[:, :1] < 8 * i, kk * jnp.exp2(jnp.minimum(ref - b, 0.0)), 0.0).astype(BF16)
                for h in range(HEADS):
                    cs = slice(h * HEAD_W, (h + 1) * HEAD_W)
                    a[n][h][i] = a[n][h][i] + lax.dot_general(
                        qt[:, cs], kt[:, cs], (((1,), (1,)), ((), ())), preferred_element_type=F32)
        a = [[jnp.concatenate(a[n][h], axis=0) if nband > 1 else a[n][h][0] for h in range(HEADS)]
             for n in range(len(items))]
        for n, ((r, q), (b, qq, kk, v_bf)) in enumerate(zip(items, pre)):
            b_last = b[c - 1:c, :]
            decay = jnp.exp2(b_last)
            qe = (qq * jnp.exp2(b)).astype(BF16)
            kd = (kk * jnp.exp2(b_last - b)).astype(BF16)
            gate = nw_t * _sigmoid(hg_ref[r, :])
            for h in range(HEADS):
                cs = slice(h * HEAD_W, (h + 1) * HEAD_W)
                i = q * HEADS + h
                a_h = jnp.where(causal, a[n][h], 0.0).astype(BF16)
                o = (jnp.dot(a_h, v_bf[:, cs], preferred_element_type=F32)
                     + lax.dot_general(qe[:, cs], st_scr[i].astype(BF16), (((1,), (1,)), ((), ())),
                                       preferred_element_type=F32))
                u_t = lax.dot_general(v_bf[:, cs], kd[:, cs], (((0,), (0,)), ((), ())),
                                      preferred_element_type=F32)
                st_scr[i] = st_scr[i] * decay[:, cs] + u_t
                rms = lax.rsqrt(jnp.mean(o * o, axis=-1, keepdims=True) + RMS_EPS)
                o_ref[r, cs] = o * rms * gate[:, cs]

    n_chunks = tb // c
    if n_chunks == 1:
        chunk_group([(pl.ds(q * tb, c), q) for q in range(nbb)])
    else:
        group = math.gcd(n_chunks, HGRN_GROUP)
        assert nbb == 1

        def trip(gi, carry):
            base = gi * (group * c)
            chunk_group([(pl.ds(pl.multiple_of(base + n * c, c), c), 0) for n in range(group)])
            return carry

        lax.fori_loop(0, n_chunks // group, trip, 0)

    @pl.when(t == pl.num_programs(1) - 1)
    def _():
        for q in range(nbb):
            for h in range(HEADS):
                s_ref[q, h] = st_scr[q * HEADS + h].T


def _hgrn(proj, blk_hq, hg_lb, hg_norm_w, s0, nb, seq, nbb, tb, c):
    nt = seq // tb
    assert nbb == 1 or nt == 1
    has_s0 = s0 is not None

    def tok(colblk):
        return pl.BlockSpec((nbb * tb, BRANCH_W), lambda b, t: (b * nt + t, colblk))

    state_spec = pl.BlockSpec((nbb, HEADS, HEAD_W, HEAD_W), lambda b, t: (b, 0, 0, 0))
    in_specs = [tok(blk_hq), tok(blk_hq + 1), tok(blk_hq + 2), tok(blk_hq + 3),
                pl.BlockSpec(hg_lb.shape, lambda b, t: (0, 0)),
                pl.BlockSpec((1, HEAD_W), lambda b, t: (0, 0))]
    args = [proj, proj, proj, proj, hg_lb, hg_norm_w]
    if has_s0:
        in_specs.append(state_spec)
        args.append(s0)
    return pl.pallas_call(
        functools.partial(_hgrn_kernel, nbb=nbb, tb=tb, c=c, has_s0=has_s0),
        grid=(nb // nbb, nt),
        in_specs=in_specs,
        out_specs=[pl.BlockSpec((nbb * tb, BRANCH_W), lambda b, t: (b * nt + t, 0)), state_spec],
        out_shape=[jax.ShapeDtypeStruct((nb * seq, BRANCH_W), F32),
                   jax.ShapeDtypeStruct((nb, HEADS, HEAD_W, HEAD_W), F32)],
        scratch_shapes=[pltpu.VMEM((nbb * HEADS, HEAD_W, HEAD_W), F32)],
        compiler_params=_cparams(("parallel", "arbitrary")),
        name="hgrn_s0" if has_s0 else "hgrn",
    )(*args)


def _lam(q1_ref, k1_ref, q2_ref, k2_ref):
    s1 = jnp.sum(q1_ref[...] * k1_ref[...], axis=-1, keepdims=True)
    s2 = jnp.sum(q2_ref[...] * k2_ref[...], axis=-1, keepdims=True)
    return jnp.exp(s1) - jnp.exp(s2) + LAM_INIT


def _da_finish(o1, o2, lam, nw):
    o = o1 - lam * o2
    rms = lax.rsqrt(jnp.mean(o * o, axis=-1, keepdims=True) + RMS_EPS)
    return o * rms * nw * (1.0 - LAM_INIT)


def _attn_prompt_kernel(q_ref, k_ref, v_ref, lq1, lk1, lq2, lk2, nw_ref, o_ref, kb, vb, *, tq, tk):
    qi = pl.program_id(2)

    @pl.when(qi == 0)
    def _():
        kb[...] = k_ref[...].astype(BF16)
        vb[...] = v_ref[...].astype(BF16)

    q = q_ref[...] * (DA_HEAD_DIM ** -0.5 * LOG2E)
    lane = lax.broadcasted_iota(jnp.int32, q.shape, 1)
    qa = jnp.where(lane < DA_HEAD_DIM, q, 0.0).astype(BF16)
    qb = jnp.where(lane >= DA_HEAD_DIM, q, 0.0).astype(BF16)
    row0 = lax.broadcasted_iota(jnp.int32, (tq, tk), 0)
    col0 = lax.broadcasted_iota(jnp.int32, (tq, tk), 1)

    def body(j, carry, masked=False):
        m1, l1, a1, m2, l2, a2 = carry
        ks = pl.ds(pl.multiple_of(j * tk, tk), tk)
        kj = kb[:, ks]
        vj = vb[ks, :]
        outs = []
        for qm, m, l, a in ((qa, m1, l1, a1), (qb, m2, l2, a2)):
            s = jnp.dot(qm, kj, preferred_element_type=F32)
            if masked:
                s = jnp.where(col0 <= row0, s, NEG)
            m_new = jnp.maximum(m, jnp.max(s, axis=-1, keepdims=True))
            corr = jnp.exp2(m - m_new)
            p = jnp.exp2(s - m_new)
            l_new = l * corr + jnp.sum(p, axis=-1, keepdims=True)
            a_new = a * corr + jnp.dot(p.astype(BF16), vj, preferred_element_type=F32)
            outs += [m_new, l_new, a_new]
        return tuple(outs)

    init = (jnp.full((tq, 1), NEG, F32), jnp.zeros((tq, 1), F32), jnp.zeros((tq, HEAD_W), F32)) * 2
    carry = lax.fori_loop(0, qi, body, init)
    m1, l1, a1, m2, l2, a2 = body(qi, carry, masked=True)
    lam = _lam(lq1, lk1, lq2, lk2)
    o_ref[...] = _da_finish(a1 / l1, a2 / l2, lam, nw_ref[...])


def _attn_prompt(proj, kt, v, lam_q1, lam_k1, lam_q2, lam_k2, da_norm_w, nb, seq, tq, tk):
    assert tq == tk
    nq = seq // tq
    small = lambda shp: pl.BlockSpec(shp, lambda b, h, i: (0, 0))
    return pl.pallas_call(
        functools.partial(_attn_prompt_kernel, tq=tq, tk=tk),
        grid=(nb, HEADS, nq),
        in_specs=[pl.BlockSpec((tq, HEAD_W), lambda b, h, i: (b * nq + i, BLK_Q * HEADS + h)),
                  pl.BlockSpec((None, HEAD_W, seq), lambda b, h, i: (b, h, 0)),
                  pl.BlockSpec((seq, HEAD_W), lambda b, h, i: (b, h)),
                  small((1, DA_HEAD_DIM)), small((1, DA_HEAD_DIM)),
                  small((1, DA_HEAD_DIM)), small((1, DA_HEAD_DIM)),
                  small((1, HEAD_W))],
        out_specs=pl.BlockSpec((tq, HEAD_W), lambda b, h, i: (b * nq + i, h)),
        out_shape=jax.ShapeDtypeStruct((nb * seq, BRANCH_W), F32),
        scratch_shapes=[pltpu.VMEM((HEAD_W, seq), BF16), pltpu.VMEM((seq, HEAD_W), BF16)],
        compiler_params=_cparams(("parallel", "parallel", "arbitrary")),
        name="attn_prompt",
    )(proj, kt, v, lam_q1, lam_k1, lam_q2, lam_k2, da_norm_w)


def _attn_sample_kernel(pt_ref, q_ref, kn_ref, vn_ref, *rest, pp, tnew):
    k_refs = rest[:pp]
    v_refs = rest[pp:2 * pp]
    lq1, lk1, lq2, lk2, nw_ref, o_ref, qblk, kcat, vcat, m_scr, l_scr, acc = rest[2 * pp:]
    g = pl.program_id(1)
    rows = HEADS * 2 * tnew

    @pl.when(g == 0)
    def _():
        q = q_ref[...] * (DA_HEAD_DIM ** -0.5)
        tiled = jnp.concatenate([q] * (HEADS * 2), axis=0)
        r = lax.broadcasted_iota(jnp.int32, tiled.shape, 0)
        cidx = lax.broadcasted_iota(jnp.int32, tiled.shape, 1)
        qblk[...] = jnp.where((r // tnew) == (cidx // DA_HEAD_DIM), tiled, 0.0).astype(BF16)
        m_scr[...] = jnp.full_like(m_scr, NEG)
        l_scr[...] = jnp.zeros_like(l_scr)
        acc[...] = jnp.zeros_like(acc)

    def update(s, kv_len):
        m_old = m_scr[...]
        m_new = jnp.maximum(m_old, jnp.max(s, axis=-1, keepdims=True))
        corr = jnp.exp(m_old - m_new)
        p = jnp.exp(s - m_new)
        l_scr[...] = l_scr[...] * corr + jnp.sum(p, axis=-1, keepdims=True)
        m_scr[...] = m_new
        p_bf = p.astype(BF16)
        for h in range(HEADS):
            rs = slice(h * 2 * tnew, (h + 1) * 2 * tnew)
            pv = jnp.dot(p_bf[rs, :], vcat[h, 0:kv_len, :], preferred_element_type=F32)
            acc[rs, :] = acc[rs, :] * corr[rs, :] + pv

    for j in range(pp):
        ls = slice(j * PAGE, (j + 1) * PAGE)
        kcat[:, ls] = k_refs[j][...].astype(BF16)
        for h in range(HEADS):
            vcat[h, ls, :] = v_refs[j][pl.ds(h, PAGE, stride=HEADS), :].astype(BF16)
    s = jnp.dot(qblk[...], kcat[...], preferred_element_type=F32)
    update(s, pp * PAGE)

    @pl.when(g == pl.num_programs(1) - 1)
    def _():
        pad = jnp.zeros((PAGE - tnew, BRANCH_W), F32)
        kn = jnp.concatenate([kn_ref[...], pad], axis=0).astype(BF16)
        vn = jnp.concatenate([vn_ref[...], pad], axis=0).astype(BF16)
        for h in range(HEADS):
            vcat[h, 0:PAGE, :] = vn[:, h * HEAD_W:(h + 1) * HEAD_W]
        sn = lax.dot_general(qblk[...], kn, (((1,), (1,)), ((), ())),
                             preferred_element_type=F32)
        r = lax.broadcasted_iota(jnp.int32, sn.shape, 0)
        cidx = lax.broadcasted_iota(jnp.int32, sn.shape, 1)
        sn = jnp.where(cidx <= (r % tnew), sn, NEG)
        update(sn, PAGE)
        lam = _lam(lq1, lk1, lq2, lk2)
        o = acc[...] / l_scr[...]
        nw = nw_ref[...]
        for h in range(HEADS):
            o1 = o[h * 2 * tnew:h * 2 * tnew + tnew, :]
            o2 = o[h * 2 * tnew + tnew:(h + 1) * 2 * tnew, :]
            o_ref[:, h * HEAD_W:(h + 1) * HEAD_W] = _da_finish(o1, o2, lam, nw)


def _attn_sample(proj, k_new, v_new, cache_k, cache_v, page_table, lam_q1, lam_k1, lam_q2, lam_k2,
                 da_norm_w, nb, tnew, pp):
    n_pages = page_table.shape[1]
    ng = n_pages // pp
    ck = cache_k.reshape(-1, PAGE, BRANCH_W).transpose(0, 2, 1)
    cv = cache_v.reshape(-1, PAGE * HEADS, HEAD_W)
    rows = HEADS * 2 * tnew

    def page_spec(j):
        return pl.BlockSpec((None, BRANCH_W, PAGE), lambda b, g, pt: (pt[b, g * pp + j], 0, 0))

    small = lambda shp: pl.BlockSpec(shp, lambda b, g, pt: (0, 0))
    in_specs = ([pl.BlockSpec((tnew, BRANCH_W), lambda b, g, pt: (b, BLK_Q)),
                 pl.BlockSpec((tnew, BRANCH_W), lambda b, g, pt: (b, 0)),
                 pl.BlockSpec((tnew, BRANCH_W), lambda b, g, pt: (b, 0))]
                + [page_spec(j) for j in range(pp)] + [page_spec(j) for j in range(pp)]
                + [small((1, DA_HEAD_DIM))] * 4 + [small((1, HEAD_W))])
    args = [page_table, proj, k_new, v_new, *([ck] * pp), *([cv] * pp),
            lam_q1, lam_k1, lam_q2, lam_k2, da_norm_w]
    scratch = [pltpu.VMEM((rows, BRANCH_W), BF16),
               pltpu.VMEM((BRANCH_W, pp * PAGE), BF16),
               pltpu.VMEM((HEADS, pp * PAGE, HEAD_W), BF16),
               pltpu.VMEM((rows, 1), F32),
               pltpu.VMEM((rows, 1), F32),
               pltpu.VMEM((rows, HEAD_W), F32)]
    return pl.pallas_call(
        functools.partial(_attn_sample_kernel, pp=pp, tnew=tnew),
        grid_spec=pltpu.PrefetchScalarGridSpec(
            num_scalar_prefetch=1, grid=(nb, ng), in_specs=in_specs,
            out_specs=pl.BlockSpec((tnew, BRANCH_W), lambda b, g, pt: (b, 0)),
            scratch_shapes=scratch),
        out_shape=jax.ShapeDtypeStruct((nb * tnew, BRANCH_W), F32),
        compiler_params=_cparams(("parallel", "arbitrary")),
        name="attn_sample",
    )(*args)


def _layernorm(x, w, b):
    mu = jnp.mean(x, axis=-1, keepdims=True)
    xc = x - mu
    var = jnp.mean(xc * xc, axis=-1, keepdims=True)
    return xc * lax.rsqrt(var + LN_EPS) * w + b


def _merge_kernel(x_ref, ohg_ref, oda_ref, ga0_ref, ga1_ref, gb0_ref, gb1_ref, wpa_ref, wpb_ref,
                  wout_ref, lw_ref, lb_ref, h_ref):
    pa = jnp.dot(ohg_ref[...].astype(BF16), wpa_ref[...], preferred_element_type=F32)
    pb = jnp.dot(oda_ref[...].astype(BF16), wpb_ref[...], preferred_element_type=F32)
    half = BRANCH_W
    merged = jnp.concatenate(
        [_sigmoid(ga0_ref[...]) * pa[:, :half] + _sigmoid(gb0_ref[...]) * pb[:, :half],
         _sigmoid(ga1_ref[...]) * pa[:, half:] + _sigmoid(gb1_ref[...]) * pb[:, half:]], axis=1)
    y = ALPHA * x_ref[...] + jnp.dot(merged.astype(BF16), wout_ref[...], preferred_element_type=F32)
    h_ref[...] = _layernorm(y, lw_ref[...], lb_ref[...])


def _merge(x, o_hg, o_da, proj, blk_gates, wpa, wpb, wout, ln_w, ln_b, tm):
    t = x.shape[0]
    rowblk = lambda w: pl.BlockSpec((tm, w), lambda i: (i, 0))
    gate = lambda c: pl.BlockSpec((tm, BRANCH_W), lambda i: (i, blk_gates + c))
    const = lambda shp: pl.BlockSpec(shp, lambda i: (0, 0), pipeline_mode=pl.Buffered(1))
    return pl.pallas_call(
        _merge_kernel,
        grid=(t // tm,),
        in_specs=[rowblk(D_MODEL), rowblk(BRANCH_W), rowblk(BRANCH_W),
                  gate(0), gate(1), gate(2), gate(3),
                  const((BRANCH_W, D_MODEL)), const((BRANCH_W, D_MODEL)), const((D_MODEL, D_MODEL)),
                  const((1, D_MODEL)), const((1, D_MODEL))],
        out_specs=rowblk(D_MODEL),
        out_shape=jax.ShapeDtypeStruct((t, D_MODEL), F32),
        compiler_params=_cparams(("parallel",)),
        name="merge_out_ln",
    )(x, o_hg, o_da, proj, proj, proj, proj, wpa, wpb, wout, ln_w, ln_b)


def _mlp_init(h_ref, y_ref, hb):
    hb[...] = h_ref[...].astype(BF16)
    y_ref[...] = jnp.zeros_like(y_ref)


def _mlp_chunk(wup, wdn, y_ref, hb):
    u = jnp.dot(hb[...], wup, preferred_element_type=F32)
    u = jnp.maximum(u, 0.0)
    y_ref[...] += jnp.dot((u * u).astype(BF16), wdn, preferred_element_type=F32)


def _mlp_finish(h_ref, lw_ref, lb_ref, y_ref):
    y_ref[...] = _layernorm(ALPHA * h_ref[...] + y_ref[...], lw_ref[...], lb_ref[...])


def _mlp_kernel(hp_ref, hs_ref, wup_ref, wdn_ref, lw_ref, lb_ref, yp_ref, ys_ref, hpb, hsb, *, n_ptiles):
    i = pl.program_id(0)
    j = pl.program_id(1)
    last = pl.num_programs(1) - 1

    def tile(h_ref, y_ref, hb):
        pl.when(j == 0)(lambda: _mlp_init(h_ref, y_ref, hb))
        _mlp_chunk(wup_ref[...], wdn_ref[...], y_ref, hb)
        pl.when(j == last)(lambda: _mlp_finish(h_ref, lw_ref, lb_ref, y_ref))

    pl.when(i < n_ptiles)(lambda: tile(hp_ref, yp_ref, hpb))
    pl.when(i == n_ptiles)(lambda: tile(hs_ref, ys_ref, hsb))


def _mlp(h_p, h_s, w_up, w_down, ln_w, ln_b, tm, tf):
    tp, ts = h_p.shape[0], h_s.shape[0]
    n_pt = tp // tm
    ptile = lambda i, j: (jnp.minimum(i, n_pt - 1), 0)
    const = lambda i, j: (0, 0)
    return pl.pallas_call(
        functools.partial(_mlp_kernel, n_ptiles=n_pt),
        grid=(n_pt + 1, D_FF // tf),
        in_specs=[pl.BlockSpec((tm, D_MODEL), ptile, pipeline_mode=pl.Buffered(1)),
                  pl.BlockSpec((ts, D_MODEL), const, pipeline_mode=pl.Buffered(1)),
                  pl.BlockSpec((D_MODEL, tf), lambda i, j: (0, j)),
                  pl.BlockSpec((tf, D_MODEL), lambda i, j: (j, 0)),
                  pl.BlockSpec((1, D_MODEL), const),
                  pl.BlockSpec((1, D_MODEL), const)],
        out_specs=[pl.BlockSpec((tm, D_MODEL), ptile), pl.BlockSpec((ts, D_MODEL), const)],
        out_shape=[jax.ShapeDtypeStruct((tp, D_MODEL), F32), jax.ShapeDtypeStruct((ts, D_MODEL), F32)],
        scratch_shapes=[pltpu.VMEM((tm, D_MODEL), BF16), pltpu.VMEM((ts, D_MODEL), BF16)],
        compiler_params=_cparams(("arbitrary", "arbitrary")),
        name="mlp_ln",
    )(h_p, h_s, w_up, w_down, ln_w, ln_b)


def kernel(x_prompt, x_sample, cache_k, cache_v, state_hgrn, page_table, w_in, hg_lb, hg_norm_w,
           lam_q1, lam_k1, lam_q2, lam_k2, da_norm_w, w_pa, w_pb, w_out, ln1_w, ln1_b,
           w_up, w_down, ln2_w, ln2_b):
    nb, seq, _ = x_prompt.shape
    db, tnew, _ = x_sample.shape
    wpa = w_pa[0].astype(BF16)
    wpb = w_pb[0].astype(BF16)
    wout = w_out[0].astype(BF16)
    lam_args = (lam_q1, lam_k1, lam_q2, lam_k2, da_norm_w)

    xp = x_prompt.reshape(nb * seq, D_MODEL)
    xs = x_sample.reshape(db * tnew, D_MODEL)
    v_p2d, kt_p, vt_p, xp_bf, k_s2d, v_s2d, xs_bf = _proj_kv(xp, xs, w_in, nb, seq, 512)
    proj_p, proj_s = _in_proj(xp_bf, xs_bf, w_in, 1024)
    o_hg_p, s_p = _hgrn(proj_p, PACKED_HQ, hg_lb, hg_norm_w, None, nb, seq, 1, 256, 16)
    o_da_p = _attn_prompt(proj_p, kt_p, v_p2d, *lam_args, nb, seq, 1024, 1024)
    h_p = _merge(xp, o_hg_p, o_da_p, proj_p, PACKED_GATES, wpa, wpb, wout, ln1_w, ln1_b, 256)
    k_p = kt_p.reshape(nb, HEADS, 2, DA_HEAD_DIM, seq).transpose(0, 4, 1, 2, 3)[None]
    v_p = vt_p.reshape(1, nb, seq, HEADS, HEAD_W)

    o_hg_s, s_s = _hgrn(proj_s, PACKED_HQ, hg_lb, hg_norm_w,
                        state_hgrn.reshape(db, HEADS, HEAD_W, HEAD_W), db, tnew, 4, tnew, tnew)
    o_da_s = _attn_sample(proj_s, k_s2d, v_s2d, cache_k, cache_v, page_table, *lam_args, db, tnew, 16)
    h_s = _merge(xs, o_hg_s, o_da_s, proj_s, PACKED_GATES, wpa, wpb, wout, ln1_w, ln1_b, 256)
    y_p, y_s = _mlp(h_p, h_s, w_up[0].astype(BF16), w_down[0].astype(BF16), ln2_w, ln2_b, 512, 1024)
    k_s = k_s2d.reshape(1, db, tnew, HEADS, 2, DA_HEAD_DIM)
    v_s = v_s2d.reshape(1, db, tnew, HEADS, HEAD_W)

    return (y_p.reshape(nb, seq, D_MODEL), y_s.reshape(db, tnew, D_MODEL),
            k_p, v_p, s_p[None], k_s, v_s, s_s[None])
```

```python
import functools
import math

import jax
import jax.numpy as jnp
from jax import lax
from jax.experimental import pallas as pl
from jax.experimental.pallas import tpu as pltpu

D_MODEL = 2048
HEADS = 8
HEAD_W = 128
DA_HEAD_DIM = 64
BRANCH_W = HEADS * HEAD_W
D_FF = 4 * D_MODEL
PAGE = 128
LN_EPS = 1e-5
RMS_EPS = 1e-6
DEPTH = 1
ALPHA = (2 * DEPTH) ** 0.25
LAM_INIT = 0.8 - 0.6 * math.exp(-0.3 * 0)
IN_WIDTH = 7 * BRANCH_W + 2 * D_MODEL
BLK_Q, BLK_K, BLK_V, BLK_HQ, BLK_GATES = 0, 1, 2, 3, 7
PACKED_HQ, PACKED_GATES = BLK_HQ - 2, BLK_GATES - 2
HGRN_GROUP = 8
LOG2E = math.log2(math.e)
NEG = -1e30

F32 = jnp.float32
BF16 = jnp.bfloat16
MIB = 1024 * 1024
VMEM_LIMIT = 52 * MIB


def _cparams(sem, vmem_limit=VMEM_LIMIT):
    return pltpu.CompilerParams(dimension_semantics=sem, vmem_limit_bytes=vmem_limit)


def _sigmoid(x):
    return 1.0 / (1.0 + jnp.exp(-x))


def _proj_kernel(xp_ref, xs_ref, w_ref, op_ref, os_ref, wb, *, n_ptiles):
    i = pl.program_id(1)

    @pl.when(i == 0)
    def _():
        wb[...] = w_ref[...].astype(BF16)

    @pl.when(i < n_ptiles)
    def _():
        op_ref[...] = jnp.dot(xp_ref[...], wb[...], preferred_element_type=F32)

    @pl.when(i == n_ptiles)
    def _():
        os_ref[...] = jnp.dot(xs_ref[...], wb[...], preferred_element_type=F32)


def _in_proj(xp_bf, xs_bf, w_in, tm):
    tp, k = xp_bf.shape
    ts = xs_bf.shape[0]
    n_pt = tp // tm
    tn = BRANCH_W
    n_blk = IN_WIDTH // tn - 2
    ptile = lambda i: jnp.minimum(i, n_pt - 1)
    return pl.pallas_call(
        functools.partial(_proj_kernel, n_ptiles=n_pt),
        grid=(n_blk, n_pt + 1),
        in_specs=[pl.BlockSpec((tm, k), lambda j, i: (ptile(i), 0)),
                  pl.BlockSpec((ts, k), lambda j, i: (0, 0)),
                  pl.BlockSpec((None, k, tn), lambda j, i: (0, 0, j + 2 * jnp.minimum(j, 1)))],
        out_specs=[pl.BlockSpec((tm, tn), lambda j, i: (ptile(i), j)),
                   pl.BlockSpec((ts, tn), lambda j, i: (0, j))],
        out_shape=[jax.ShapeDtypeStruct((tp, n_blk * tn), F32),
                   jax.ShapeDtypeStruct((ts, n_blk * tn), F32)],
        scratch_shapes=[pltpu.VMEM((k, tn), BF16)],
        compiler_params=_cparams(("arbitrary", "arbitrary")),
        name="in_proj",
    )(xp_bf, xs_bf, w_in)


def _proj_kv_kernel(xp_ref, xs_ref, wk_ref, wv_ref, v_ref, kt_ref, vt_ref, xpb_ref,
                    ks_ref, vs_ref, xsb_ref, wkb, wvb, *, n_ptiles):
    i = pl.program_id(0)

    @pl.when(i == 0)
    def _():
        wkb[...] = wk_ref[...].astype(BF16)
        wvb[...] = wv_ref[...].astype(BF16)

    @pl.when(i < n_ptiles)
    def _():
        x = xp_ref[...].astype(BF16)
        xpb_ref[...] = x
        tm = x.shape[0]
        kt_ref[...] = jnp.dot(x, wkb[...], preferred_element_type=F32).T
        v = jnp.dot(x, wvb[...], preferred_element_type=F32)
        v_ref[...] = v
        for h in range(HEADS):
            vt_ref[pl.ds(h, tm, stride=HEADS), :] = v[:, h * HEAD_W:(h + 1) * HEAD_W]

    @pl.when(i == n_ptiles)
    def _():
        x = xs_ref[...].astype(BF16)
        xsb_ref[...] = x
        ks_ref[...] = jnp.dot(x, wkb[...], preferred_element_type=F32)
        vs_ref[...] = jnp.dot(x, wvb[...], preferred_element_type=F32)


def _proj_kv(xp, xs, w_in, nb, seq, tm):
    tp, k = xp.shape
    ts = xs.shape[0]
    nt = seq // tm
    n_pt = tp // tm
    pt = lambda i: jnp.minimum(i, n_pt - 1)
    const = lambda i: (0, 0)
    wspec = lambda c: pl.BlockSpec((None, k, BRANCH_W), lambda i: (0, 0, c), pipeline_mode=pl.Buffered(1))
    return pl.pallas_call(
        functools.partial(_proj_kv_kernel, n_ptiles=n_pt),
        grid=(n_pt + 1,),
        in_specs=[pl.BlockSpec((tm, k), lambda i: (pt(i), 0)),
                  pl.BlockSpec((ts, k), const, pipeline_mode=pl.Buffered(1)),
                  wspec(BLK_K), wspec(BLK_V)],
        out_specs=[pl.BlockSpec((tm, BRANCH_W), lambda i: (pt(i), 0)),
                   pl.BlockSpec((None, BRANCH_W, tm), lambda i: (pt(i) // nt, 0, pt(i) % nt)),
                   pl.BlockSpec((tm * HEADS, HEAD_W), lambda i: (pt(i), 0)),
                   pl.BlockSpec((tm, k), lambda i: (pt(i), 0)),
                   pl.BlockSpec((ts, BRANCH_W), const),
                   pl.BlockSpec((ts, BRANCH_W), const),
                   pl.BlockSpec((ts, k), const)],
        out_shape=[jax.ShapeDtypeStruct((tp, BRANCH_W), F32),
                   jax.ShapeDtypeStruct((nb, BRANCH_W, seq), F32),
                   jax.ShapeDtypeStruct((tp * HEADS, HEAD_W), F32),
                   jax.ShapeDtypeStruct((tp, k), BF16),
                   jax.ShapeDtypeStruct((ts, BRANCH_W), F32),
                   jax.ShapeDtypeStruct((ts, BRANCH_W), F32),
                   jax.ShapeDtypeStruct((ts, k), BF16)],
        scratch_shapes=[pltpu.VMEM((k, BRANCH_W), BF16), pltpu.VMEM((k, BRANCH_W), BF16)],
        compiler_params=_cparams(("arbitrary",), vmem_limit=56 * MIB),
        name="proj_kv",
    )(xp, xs, w_in, w_in)


def _hgrn_kernel(*refs, nbb, tb, c, has_s0):
    if has_s0:
        hq_ref, hf_ref, hi_ref, hg_ref, lb_ref, nw_ref, s0_ref, o_ref, s_ref, st_scr = refs
    else:
        hq_ref, hf_ref, hi_ref, hg_ref, lb_ref, nw_ref, o_ref, s_ref, st_scr = refs
        s0_ref = None
    t = pl.program_id(1)

    @pl.when(t == 0)
    def _():
        for q in range(nbb):
            for h in range(HEADS):
                if has_s0:
                    st_scr[q * HEADS + h] = s0_ref[q, h].T
                else:
                    st_scr[q * HEADS + h] = jnp.zeros((HEAD_W, HEAD_W), F32)

    lbx = lb_ref[...]
    lbe = jnp.exp(lbx - jnp.max(lbx, axis=0, keepdims=True))
    lb_all = lbe[0:1, :] / jnp.sum(lbe, axis=0, keepdims=True)
    nw_t = jnp.concatenate([nw_ref[...]] * HEADS, axis=1)

    row = lax.broadcasted_iota(jnp.int32, (c, c), 0)
    col = lax.broadcasted_iota(jnp.int32, (c, c), 1)
    tri = jnp.where(row >= col, 1.0, 0.0).astype(BF16)
    causal = row >= col

    one_m_lb = 1.0 - lb_all

    def chunk_group(items):
        pre = []
        for r, _ in items:
            z = hf_ref[r, :]
            logf = jnp.log2(lb_all + one_m_lb * _sigmoid(z))
            kk = one_m_lb * _sigmoid(-z)
            hq = hq_ref[r, :]
            qq = hq * _sigmoid(hq)
            v_bf = hi_ref[r, :].astype(BF16)
            p0 = logf.astype(BF16)
            r1 = logf - p0.astype(F32)
            p1 = r1.astype(BF16)
            p2 = (r1 - p1.astype(F32)).astype(BF16)
            b = (jnp.dot(tri, p0, preferred_element_type=F32)
                 + jnp.dot(tri, p1, preferred_element_type=F32)
                 + jnp.dot(tri, p2, preferred_element_type=F32))
            pre.append((b, qq, kk, v_bf))
        nband = c // 8
        a = [[[jnp.zeros((8, c), F32) for _ in range(nband)] for _ in range(HEADS)] for _ in items]
        for s in range(c):
            i, lo = s // 8, (s // 8) * 8
            for n, (b, qq, kk, _) in enumerate(pre):
                d = jnp.minimum(b[lo:lo + 8, :] - b[s:s + 1, :], 0.0)
                term = qq[lo:lo + 8, :] * kk[s:s + 1, :] * jnp.exp2(d)
                for h in range(HEADS):
                    colsum = jnp.sum(term[:, h * HEAD_W:(h + 1) * HEAD_W], axis=-1, keepdims=True)
                    a[n][h][i] = jnp.where(col[:8] == s, colsum, a[n][h][i])
        for n, (b, qq, kk, _) in enumerate(pre):
            for i in range(1, nband):
                ref = b[8 * i:8 * i + 1, :]
                qt = (qq[8 * i:8 * i + 8, :] * jnp.exp2(b[8 * i:8 * i + 8, :] - ref)).astype(BF16)
                kt = jnp.where(row[:, :1] < 8 * i, kk * jnp.exp2(jnp.minimum(ref - b, 0.0)), 0.0).astype(BF16)
                for h in range(HEADS):
                    cs = slice(h * HEAD_W, (h + 1) * HEAD_W)
                    a[n][h][i] = a[n][h][i] + lax.dot_general(
                        qt[:, cs], kt[:, cs], (((1,), (1,)), ((), ())), preferred_element_type=F32)
        a = [[jnp.concatenate(a[n][h], axis=0) if nband > 1 else a[n][h][0] for h in range(HEADS)]
             for n in range(len(items))]
        for n, ((r, q), (b, qq, kk, v_bf)) in enumerate(zip(items, pre)):
            b_last = b[c - 1:c, :]
            decay = jnp.exp2(b_last)
            qe = (qq * jnp.exp2(b)).astype(BF16)
            kd = (kk * jnp.exp2(b_last - b)).astype(BF16)
            gate = nw_t * _sigmoid(hg_ref[r, :])
            for h in range(HEADS):
                cs = slice(h * HEAD_W, (h + 1) * HEAD_W)
                i = q * HEADS + h
                a_h = jnp.where(causal, a[n][h], 0.0).astype(BF16)
                o = (jnp.dot(a_h, v_bf[:, cs], preferred_element_type=F32)
                     + lax.dot_general(qe[:, cs], st_scr[i].astype(BF16), (((1,), (1,)), ((), ())),
                                       preferred_element_type=F32))
                u_t = lax.dot_general(v_bf[:, cs], kd[:, cs], (((0,), (0,)), ((), ())),
                                      preferred_element_type=F32)
                st_scr[i] = st_scr[i] * decay[:, cs] + u_t
                rms = lax.rsqrt(jnp.mean(o * o, axis=-1, keepdims=True) + RMS_EPS)
                o_ref[r, cs] = o * rms * gate[:, cs]

    n_chunks = tb // c
    if n_chunks == 1:
        chunk_group([(pl.ds(q * tb, c), q) for q in range(nbb)])
    else:
        group = math.gcd(n_chunks, HGRN_GROUP)
        assert nbb == 1

        def trip(gi, carry):
            base = gi * (group * c)
            chunk_group([(pl.ds(pl.multiple_of(base + n * c, c), c), 0) for n in range(group)])
            return carry

        lax.fori_loop(0, n_chunks // group, trip, 0)

    @pl.when(t == pl.num_programs(1) - 1)
    def _():
        for q in range(nbb):
            for h in range(HEADS):
                s_ref[q, h] = st_scr[q * HEADS + h].T


def _hgrn(proj, blk_hq, hg_lb, hg_norm_w, s0, nb, seq, nbb, tb, c):
    nt = seq // tb
    assert nbb == 1 or nt == 1
    has_s0 = s0 is not None

    def tok(colblk):
        return pl.BlockSpec((nbb * tb, BRANCH_W), lambda b, t: (b * nt + t, colblk))

    state_spec = pl.BlockSpec((nbb, HEADS, HEAD_W, HEAD_W), lambda b, t: (b, 0, 0, 0))
    in_specs = [tok(blk_hq), tok(blk_hq + 1), tok(blk_hq + 2), tok(blk_hq + 3),
                pl.BlockSpec(hg_lb.shape, lambda b, t: (0, 0)),
                pl.BlockSpec((1, HEAD_W), lambda b, t: (0, 0))]
    args = [proj, proj, proj, proj, hg_lb, hg_norm_w]
    if has_s0:
        in_specs.append(state_spec)
        args.append(s0)
    return pl.pallas_call(
        functools.partial(_hgrn_kernel, nbb=nbb, tb=tb, c=c, has_s0=has_s0),
        grid=(nb // nbb, nt),
        in_specs=in_specs,
        out_specs=[pl.BlockSpec((nbb * tb, BRANCH_W), lambda b, t: (b * nt + t, 0)), state_spec],
        out_shape=[jax.ShapeDtypeStruct((nb * seq, BRANCH_W), F32),
                   jax.ShapeDtypeStruct((nb, HEADS, HEAD_W, HEAD_W), F32)],
        scratch_shapes=[pltpu.VMEM((nbb * HEADS, HEAD_W, HEAD_W), F32)],
        compiler_params=_cparams(("parallel", "arbitrary")),
        name="hgrn_s0" if has_s0 else "hgrn",
    )(*args)


def _lam(q1_ref, k1_ref, q2_ref, k2_ref):
    s1 = jnp.sum(q1_ref[...] * k1_ref[...], axis=-1, keepdims=True)
    s2 = jnp.sum(q2_ref[...] * k2_ref[...], axis=-1, keepdims=True)
    return jnp.exp(s1) - jnp.exp(s2) + LAM_INIT


def _da_finish(o1, o2, lam, nw):
    o = o1 - lam * o2
    rms = lax.rsqrt(jnp.mean(o * o, axis=-1, keepdims=True) + RMS_EPS)
    return o * rms * nw * (1.0 - LAM_INIT)


def _attn_prompt_kernel(q_ref, k_ref, v_ref, lq1, lk1, lq2, lk2, nw_ref, o_ref, kb, vb, *, tq, tk):
    qi = pl.program_id(2)

    @pl.when(qi == 0)
    def _():
        kb[...] = k_ref[...].astype(BF16)
        vb[...] = v_ref[...].astype(BF16)

    q = q_ref[...] * (DA_HEAD_DIM ** -0.5 * LOG2E)
    lane = lax.broadcasted_iota(jnp.int32, q.shape, 1)
    qa = jnp.where(lane < DA_HEAD_DIM, q, 0.0).astype(BF16)
    qb = jnp.where(lane >= DA_HEAD_DIM, q, 0.0).astype(BF16)
    row0 = lax.broadcasted_iota(jnp.int32, (tq, tk), 0)
    col0 = lax.broadcasted_iota(jnp.int32, (tq, tk), 1)

    def body(j, carry, masked=False):
        m1, l1, a1, m2, l2, a2 = carry
        ks = pl.ds(pl.multiple_of(j * tk, tk), tk)
        kj = kb[:, ks]
        vj = vb[ks, :]
        outs = []
        for qm, m, l, a in ((qa, m1, l1, a1), (qb, m2, l2, a2)):
            s = jnp.dot(qm, kj, preferred_element_type=F32)
            if masked:
                s = jnp.where(col0 <= row0, s, NEG)
            m_new = jnp.maximum(m, jnp.max(s, axis=-1, keepdims=True))
            corr = jnp.exp2(m - m_new)
            p = jnp.exp2(s - m_new)
            l_new = l * corr + jnp.sum(p, axis=-1, keepdims=True)
            a_new = a * corr + jnp.dot(p.astype(BF16), vj, preferred_element_type=F32)
            outs += [m_new, l_new, a_new]
        return tuple(outs)

    init = (jnp.full((tq, 1), NEG, F32), jnp.zeros((tq, 1), F32), jnp.zeros((tq, HEAD_W), F32)) * 2
    carry = lax.fori_loop(0, qi, body, init)
    m1, l1, a1, m2, l2, a2 = body(qi, carry, masked=True)
    lam = _lam(lq1, lk1, lq2, lk2)
    o_ref[...] = _da_finish(a1 / l1, a2 / l2, lam, nw_ref[...])


def _attn_prompt(proj, kt, v, lam_q1, lam_k1, lam_q2, lam_k2, da_norm_w, nb, seq, tq, tk):
    assert tq == tk
    nq = seq // tq
    small = lambda shp: pl.BlockSpec(shp, lambda b, h, i: (0, 0))
    return pl.pallas_call(
        functools.partial(_attn_prompt_kernel, tq=tq, tk=tk),
        grid=(nb, HEADS, nq),
        in_specs=[pl.BlockSpec((tq, HEAD_W), lambda b, h, i: (b * nq + i, BLK_Q * HEADS + h)),
                  pl.BlockSpec((None, HEAD_W, seq), lambda b, h, i: (b, h, 0)),
                  pl.BlockSpec((seq, HEAD_W), lambda b, h, i: (b, h)),
                  small((1, DA_HEAD_DIM)), small((1, DA_HEAD_DIM)),
                  small((1, DA_HEAD_DIM)), small((1, DA_HEAD_DIM)),
                  small((1, HEAD_W))],
        out_specs=pl.BlockSpec((tq, HEAD_W), lambda b, h, i: (b * nq + i, h)),
        out_shape=jax.ShapeDtypeStruct((nb * seq, BRANCH_W), F32),
        scratch_shapes=[pltpu.VMEM((HEAD_W, seq), BF16), pltpu.VMEM((seq, HEAD_W), BF16)],
        compiler_params=_cparams(("parallel", "parallel", "arbitrary")),
        name="attn_prompt",
    )(proj, kt, v, lam_q1, lam_k1, lam_q2, lam_k2, da_norm_w)


def _attn_sample_kernel(pt_ref, q_ref, kn_ref, vn_ref, *rest, pp, tnew):
    k_refs = rest[:pp]
    v_refs = rest[pp:2 * pp]
    lq1, lk1, lq2, lk2, nw_ref, o_ref, qblk, kcat, vcat, m_scr, l_scr, acc = rest[2 * pp:]
    g = pl.program_id(1)
    rows = HEADS * 2 * tnew

    @pl.when(g == 0)
    def _():
        q = q_ref[...] * (DA_HEAD_DIM ** -0.5)
        tiled = jnp.concatenate([q] * (HEADS * 2), axis=0)
        r = lax.broadcasted_iota(jnp.int32, tiled.shape, 0)
        cidx = lax.broadcasted_iota(jnp.int32, tiled.shape, 1)
        qblk[...] = jnp.where((r // tnew) == (cidx // DA_HEAD_DIM), tiled, 0.0).astype(BF16)
        m_scr[...] = jnp.full_like(m_scr, NEG)
        l_scr[...] = jnp.zeros_like(l_scr)
        acc[...] = jnp.zeros_like(acc)

    def update(s, kv_len):
        m_old = m_scr[...]
        m_new = jnp.maximum(m_old, jnp.max(s, axis=-1, keepdims=True))
        corr = jnp.exp(m_old - m_new)
        p = jnp.exp(s - m_new)
        l_scr[...] = l_scr[...] * corr + jnp.sum(p, axis=-1, keepdims=True)
        m_scr[...] = m_new
        p_bf = p.astype(BF16)
        for h in range(HEADS):
            rs = slice(h * 2 * tnew, (h + 1) * 2 * tnew)
            pv = jnp.dot(p_bf[rs, :], vcat[h, 0:kv_len, :], preferred_element_type=F32)
            acc[rs, :] = acc[rs, :] * corr[rs, :] + pv

    for j in range(pp):
        ls = slice(j * PAGE, (j + 1) * PAGE)
        kcat[:, ls] = k_refs[j][...].astype(BF16)
        for h in range(HEADS):
            vcat[h, ls, :] = v_refs[j][pl.ds(h, PAGE, stride=HEADS), :].astype(BF16)
    s = jnp.dot(qblk[...], kcat[...], preferred_element_type=F32)
    update(s, pp * PAGE)

    @pl.when(g == pl.num_programs(1) - 1)
    def _():
        pad = jnp.zeros((PAGE - tnew, BRANCH_W), F32)
        kn = jnp.concatenate([kn_ref[...], pad], axis=0).astype(BF16)
        vn = jnp.concatenate([vn_ref[...], pad], axis=0).astype(BF16)
        for h in range(HEADS):
            vcat[h, 0:PAGE, :] = vn[:, h * HEAD_W:(h + 1) * HEAD_W]
        sn = lax.dot_general(qblk[...], kn, (((1,), (1,)), ((), ())),
                             preferred_element_type=F32)
        r = lax.broadcasted_iota(jnp.int32, sn.shape, 0)
        cidx = lax.broadcasted_iota(jnp.int32, sn.shape, 1)
        sn = jnp.where(cidx <= (r % tnew), sn, NEG)
        update(sn, PAGE)
        lam = _lam(lq1, lk1, lq2, lk2)
        o = acc[...] / l_scr[...]
        nw = nw_ref[...]
        for h in range(HEADS):
            o1 = o[h * 2 * tnew:h * 2 * tnew + tnew, :]
            o2 = o[h * 2 * tnew + tnew:(h + 1) * 2 * tnew, :]
            o_ref[:, h * HEAD_W:(h + 1) * HEAD_W] = _da_finish(o1, o2, lam, nw)


def _attn_sample(proj, k_new, v_new, cache_k, cache_v, page_table, lam_q1, lam_k1, lam_q2, lam_k2,
                 da_norm_w, nb, tnew, pp):
    n_pages = page_table.shape[1]
    ng = n_pages // pp
    ck = cache_k.reshape(-1, PAGE, BRANCH_W).transpose(0, 2, 1)
    cv = cache_v.reshape(-1, PAGE * HEADS, HEAD_W)
    rows = HEADS * 2 * tnew

    def page_spec(j):
        return pl.BlockSpec((None, BRANCH_W, PAGE), lambda b, g, pt: (pt[b, g * pp + j], 0, 0))

    small = lambda shp: pl.BlockSpec(shp, lambda b, g, pt: (0, 0))
    in_specs = ([pl.BlockSpec((tnew, BRANCH_W), lambda b, g, pt: (b, BLK_Q)),
                 pl.BlockSpec((tnew, BRANCH_W), lambda b, g, pt: (b, 0)),
                 pl.BlockSpec((tnew, BRANCH_W), lambda b, g, pt: (b, 0))]
                + [page_spec(j) for j in range(pp)] + [page_spec(j) for j in range(pp)]
                + [small((1, DA_HEAD_DIM))] * 4 + [small((1, HEAD_W))])
    args = [page_table, proj, k_new, v_new, *([ck] * pp), *([cv] * pp),
            lam_q1, lam_k1, lam_q2, lam_k2, da_norm_w]
    scratch = [pltpu.VMEM((rows, BRANCH_W), BF16),
               pltpu.VMEM((BRANCH_W, pp * PAGE), BF16),
               pltpu.VMEM((HEADS, pp * PAGE, HEAD_W), BF16),
               pltpu.VMEM((rows, 1), F32),
               pltpu.VMEM((rows, 1), F32),
               pltpu.VMEM((rows, HEAD_W), F32)]
    return pl.pallas_call(
        functools.partial(_attn_sample_kernel, pp=pp, tnew=tnew),
        grid_spec=pltpu.PrefetchScalarGridSpec(
            num_scalar_prefetch=1, grid=(nb, ng), in_specs=in_specs,
            out_specs=pl.BlockSpec((tnew, BRANCH_W), lambda b, g, pt: (b, 0)),
            scratch_shapes=scratch),
        out_shape=jax.ShapeDtypeStruct((nb * tnew, BRANCH_W), F32),
        compiler_params=_cparams(("parallel", "arbitrary")),
        name="attn_sample",
    )(*args)


def _layernorm(x, w, b):
    mu = jnp.mean(x, axis=-1, keepdims=True)
    xc = x - mu
    var = jnp.mean(xc * xc, axis=-1, keepdims=True)
    return xc * lax.rsqrt(var + LN_EPS) * w + b


def _merge_kernel(x_ref, ohg_ref, oda_ref, ga0_ref, ga1_ref, gb0_ref, gb1_ref, wpa_ref, wpb_ref,
                  wout_ref, lw_ref, lb_ref, h_ref):
    pa = jnp.dot(ohg_ref[...].astype(BF16), wpa_ref[...], preferred_element_type=F32)
    pb = jnp.dot(oda_ref[...].astype(BF16), wpb_ref[...], preferred_element_type=F32)
    half = BRANCH_W
    merged = jnp.concatenate(
        [_sigmoid(ga0_ref[...]) * pa[:, :half] + _sigmoid(gb0_ref[...]) * pb[:, :half],
         _sigmoid(ga1_ref[...]) * pa[:, half:] + _sigmoid(gb1_ref[...]) * pb[:, half:]], axis=1)
    y = ALPHA * x_ref[...] + jnp.dot(merged.astype(BF16), wout_ref[...], preferred_element_type=F32)
    h_ref[...] = _layernorm(y, lw_ref[...], lb_ref[...])


def _merge(x, o_hg, o_da, proj, blk_gates, wpa, wpb, wout, ln_w, ln_b, tm):
    t = x.shape[0]
    rowblk = lambda w: pl.BlockSpec((tm, w), lambda i: (i, 0))
    gate = lambda c: pl.BlockSpec((tm, BRANCH_W), lambda i: (i, blk_gates + c))
    const = lambda shp: pl.BlockSpec(shp, lambda i: (0, 0), pipeline_mode=pl.Buffered(1))
    return pl.pallas_call(
        _merge_kernel,
        grid=(t // tm,),
        in_specs=[rowblk(D_MODEL), rowblk(BRANCH_W), rowblk(BRANCH_W),
                  gate(0), gate(1), gate(2), gate(3),
                  const((BRANCH_W, D_MODEL)), const((BRANCH_W, D_MODEL)), const((D_MODEL, D_MODEL)),
                  const((1, D_MODEL)), const((1, D_MODEL))],
        out_specs=rowblk(D_MODEL),
        out_shape=jax.ShapeDtypeStruct((t, D_MODEL), F32),
        compiler_params=_cparams(("parallel",)),
        name="merge_out_ln",
    )(x, o_hg, o_da, proj, proj, proj, proj, wpa, wpb, wout, ln_w, ln_b)


def _mlp_init(h_ref, y_ref, hb):
    hb[...] = h_ref[...].astype(BF16)
    y_ref[...] = jnp.zeros_like(y_ref)


def _mlp_chunk(wup, wdn, y_ref, hb):
    u = jnp.dot(hb[...], wup, preferred_element_type=F32)
    u = jnp.maximum(u, 0.0)
    y_ref[...] += jnp.dot((u * u).astype(BF16), wdn, preferred_element_type=F32)


def _mlp_finish(h_ref, lw_ref, lb_ref, y_ref):
    y_ref[...] = _layernorm(ALPHA * h_ref[...] + y_ref[...], lw_ref[...], lb_ref[...])


def _mlp_kernel(hp_ref, hs_ref, wup_ref, wdn_ref, lw_ref, lb_ref, yp_ref, ys_ref, hpb, hsb, *, n_ptiles):
    i = pl.program_id(0)
    j = pl.program_id(1)
    last = pl.num_programs(1) - 1

    def tile(h_ref, y_ref, hb):
        pl.when(j == 0)(lambda: _mlp_init(h_ref, y_ref, hb))
        _mlp_chunk(wup_ref[...], wdn_ref[...], y_ref, hb)
        pl.when(j == last)(lambda: _mlp_finish(h_ref, lw_ref, lb_ref, y_ref))

    pl.when(i < n_ptiles)(lambda: tile(hp_ref, yp_ref, hpb))
    pl.when(i == n_ptiles)(lambda: tile(hs_ref, ys_ref, hsb))


def _mlp(h_p, h_s, w_up, w_down, ln_w, ln_b, tm, tf):
    tp, ts = h_p.shape[0], h_s.shape[0]
    n_pt = tp // tm
    ptile = lambda i, j: (jnp.minimum(i, n_pt - 1), 0)
    const = lambda i, j: (0, 0)
    return pl.pallas_call(
        functools.partial(_mlp_kernel, n_ptiles=n_pt),
        grid=(n_pt + 1, D_FF // tf),
        in_specs=[pl.BlockSpec((tm, D_MODEL), ptile, pipeline_mode=pl.Buffered(1)),
                  pl.BlockSpec((ts, D_MODEL), const, pipeline_mode=pl.Buffered(1)),
                  pl.BlockSpec((D_MODEL, tf), lambda i, j: (0, j)),
                  pl.BlockSpec((tf, D_MODEL), lambda i, j: (j, 0)),
                  pl.BlockSpec((1, D_MODEL), const),
                  pl.BlockSpec((1, D_MODEL), const)],
        out_specs=[pl.BlockSpec((tm, D_MODEL), ptile), pl.BlockSpec((ts, D_MODEL), const)],
        out_shape=[jax.ShapeDtypeStruct((tp, D_MODEL), F32), jax.ShapeDtypeStruct((ts, D_MODEL), F32)],
        scratch_shapes=[pltpu.VMEM((tm, D_MODEL), BF16), pltpu.VMEM((ts, D_MODEL), BF16)],
        compiler_params=_cparams(("arbitrary", "arbitrary")),
        name="mlp_ln",
    )(h_p, h_s, w_up, w_down, ln_w, ln_b)


def kernel(x_prompt, x_sample, cache_k, cache_v, state_hgrn, page_table, w_in, hg_lb, hg_norm_w,
           lam_q1, lam_k1, lam_q2, lam_k2, da_norm_w, w_pa, w_pb, w_out, ln1_w, ln1_b,
           w_up, w_down, ln2_w, ln2_b):
    nb, seq, _ = x_prompt.shape
    db, tnew, _ = x_sample.shape
    wpa = w_pa[0].astype(BF16)
    wpb = w_pb[0].astype(BF16)
    wout = w_out[0].astype(BF16)
    lam_args = (lam_q1, lam_k1, lam_q2, lam_k2, da_norm_w)

    xp = x_prompt.reshape(nb * seq, D_MODEL)
    xs = x_sample.reshape(db * tnew, D_MODEL)
    v_p2d, kt_p, vt_p, xp_bf, k_s2d, v_s2d, xs_bf = _proj_kv(xp, xs, w_in, nb, seq, 512)
    proj_p, proj_s = _in_proj(xp_bf, xs_bf, w_in, 1024)
    o_hg_p, s_p = _hgrn(proj_p, PACKED_HQ, hg_lb, hg_norm_w, None, nb, seq, 1, 256, 32)
    o_da_p = _attn_prompt(proj_p, kt_p, v_p2d, *lam_args, nb, seq, 1024, 1024)
    h_p = _merge(xp, o_hg_p, o_da_p, proj_p, PACKED_GATES, wpa, wpb, wout, ln1_w, ln1_b, 256)
    k_p = kt_p.reshape(nb, HEADS, 2, DA_HEAD_DIM, seq).transpose(0, 4, 1, 2, 3)[None]
    v_p = vt_p.reshape(1, nb, seq, HEADS, HEAD_W)

    o_hg_s, s_s = _hgrn(proj_s, PACKED_HQ, hg_lb, hg_norm_w,
                        state_hgrn.reshape(db, HEADS, HEAD_W, HEAD_W), db, tnew, 4, tnew, tnew)
    o_da_s = _attn_sample(proj_s, k_s2d, v_s2d, cache_k, cache_v, page_table, *lam_args, db, tnew, 16)
    h_s = _merge(xs, o_hg_s, o_da_s, proj_s, PACKED_GATES, wpa, wpb, wout, ln1_w, ln1_b, 256)
    y_p, y_s = _mlp(h_p, h_s, w_up[0].astype(BF16), w_down[0].astype(BF16), ln2_w, ln2_b, 512, 1024)
    k_s = k_s2d.reshape(1, db, tnew, HEADS, 2, DA_HEAD_DIM)
    v_s = v_s2d.reshape(1, db, tnew, HEADS, HEAD_W)

    return (y_p.reshape(nb, seq, D_MODEL), y_s.reshape(db, tnew, D_MODEL),
            k_p, v_p, s_p[None], k_s, v_s, s_s[None])
```
